```python
import jax, jax.numpy as jnp
from jax import lax
import numpy as np

D_MODEL = 1024
BATCH = 16
SEQ = 2048
DEPTH = 4

HEAD_DIM = 64
N_HEADS = D_MODEL // HEAD_DIM
N_FOX_HEADS = N_HEADS // 2
N_MOBA_HEADS = N_HEADS - N_FOX_HEADS
FOX_WIDTH = N_FOX_HEADS * HEAD_DIM
MOBA_WIDTH = N_MOBA_HEADS * HEAD_DIM
MIX_WIDTH = FOX_WIDTH + MOBA_WIDTH
IN_SECTIONS = (FOX_WIDTH, FOX_WIDTH, FOX_WIDTH, N_FOX_HEADS, MOBA_WIDTH, MOBA_WIDTH, MOBA_WIDTH)
IN_COLS = sum(IN_SECTIONS)
D_FF = ((8 * D_MODEL // 3 + 127) // 128) * 128
Q_BLOCK = 128
MOBA_BLOCK = 256
MOBA_TOPK = 3
RMS_EPS = 1e-6
NEG_INF = -1e30
FGATE_BIAS_CENTER = 3.0

kernel_name = "fox_moba_macaron_hybrid"


def rms_norm(x, g):
    xf = x.astype(jnp.float32)
    y = xf * lax.rsqrt(jnp.mean(xf * xf, axis=-1, keepdims=True) + RMS_EPS)
    return (y * g.astype(jnp.float32)).astype(x.dtype)


def swiglu_ffn(x, g, w_gu, w_down):
    h = rms_norm(x, g)
    gate, up = jnp.split(h @ w_gu, 2, axis=-1)
    return (jax.nn.silu(gate) * up) @ w_down


def split_heads(t, n_heads):
    b, s, _ = t.shape
    return t.reshape(b, s, n_heads, HEAD_DIM).transpose(0, 2, 1, 3)


def merge_heads(t):
    b, h, s, d = t.shape
    return t.transpose(0, 2, 1, 3).reshape(b, s, h * d)


def forgetting_attention(q, k, v, log_f):
    seq = q.shape[2]
    cum = jnp.cumsum(log_f, axis=-1)
    scale = HEAD_DIM ** -0.5
    outs = []
    for start in range(0, seq, Q_BLOCK):
        end = start + Q_BLOCK
        qb = q[:, :, start:end]
        kb = k[:, :, :end]
        vb = v[:, :, :end]
        s = jnp.einsum('bhqd,bhkd->bhqk', qb, kb).astype(jnp.float32) * scale
        s = s + cum[:, :, start:end, None] - cum[:, :, None, :end]
        t_pos = jnp.arange(start, end)
        s_pos = jnp.arange(end)
        s = jnp.where(s_pos[None, :] <= t_pos[:, None], s, NEG_INF)
        p = jax.nn.softmax(s, axis=-1).astype(v.dtype)
        outs.append(jnp.einsum('bhqk,bhkd->bhqd', p, vb))
    return jnp.concatenate(outs, axis=2)


def moba_attention(q, k, v, slopes):
    b, h, seq, dh = q.shape
    n_full = seq // MOBA_BLOCK
    scale = HEAD_DIM ** -0.5
    k_blocks = k[:, :, :n_full * MOBA_BLOCK].reshape(b, h, n_full, MOBA_BLOCK, dh)
    v_blocks = v[:, :, :n_full * MOBA_BLOCK].reshape(b, h, n_full, MOBA_BLOCK, dh)
    k_mean = jnp.mean(k_blocks.astype(jnp.float32), axis=3)
    b_idx = jnp.arange(b)[:, None, None, None]
    h_idx = jnp.arange(h)[None, :, None, None]
    slope = slopes.astype(jnp.float32)[None, :, None, None]
    in_block = jnp.arange(MOBA_BLOCK)
    outs = []
    for start in range(0, seq, Q_BLOCK):
        end = start + Q_BLOCK
        blk = start // MOBA_BLOCK
        blk_start = blk * MOBA_BLOCK
        qb = q[:, :, start:end]
        t_pos = jnp.arange(start, end)
        ko = k[:, :, blk_start:end]
        vo = v[:, :, blk_start:end]
        so = jnp.einsum('bhqd,bhkd->bhqk', qb, ko).astype(jnp.float32) * scale
        dist_o = (t_pos[:, None] - jnp.arange(blk_start, end)[None, :])
        so = so - slope * dist_o.astype(jnp.float32)
        so = jnp.where(dist_o >= 0, so, NEG_INF)
        n_sel = min(MOBA_TOPK, blk)
        if n_sel == 0:
            p = jax.nn.softmax(so, axis=-1).astype(v.dtype)
            outs.append(jnp.einsum('bhqk,bhkd->bhqd', p, vo))
            continue
        gate = jnp.einsum('bhqd,bhnd->bhqn', qb.astype(jnp.float32), k_mean[:, :, :blk])
        _, idx = lax.top_k(gate, n_sel)
        k_sel = k_blocks[b_idx, h_idx, idx].reshape(b, h, Q_BLOCK, n_sel * MOBA_BLOCK, dh)
        v_sel = v_blocks[b_idx, h_idx, idx].reshape(b, h, Q_BLOCK, n_sel * MOBA_BLOCK, dh)
        pos_sel = (idx[..., None] * MOBA_BLOCK + in_block).reshape(b, h, Q_BLOCK, n_sel * MOBA_BLOCK)
        ss = jnp.einsum('bhqd,bhqkd->bhqk', qb, k_sel).astype(jnp.float32) * scale
        ss = ss - slope * (t_pos[None, None, :, None] - pos_sel).astype(jnp.float32)
        p = jax.nn.softmax(jnp.concatenate([ss, so], axis=-1), axis=-1).astype(v.dtype)
        p_sel, p_own = p[..., :n_sel * MOBA_BLOCK], p[..., n_sel * MOBA_BLOCK:]
        out = jnp.einsum('bhqk,bhqkd->bhqd', p_sel, v_sel) + jnp.einsum('bhqk,bhkd->bhqd', p_own, vo)
        outs.append(out)
    return jnp.concatenate(outs, axis=2)


def setup_inputs(seed: int = 0) -> dict:
    key = jax.random.key(seed)
    ks = jax.random.split(key, 16)
    L, D = DEPTH, D_MODEL

    def nrm(k, shape, fan_in):
        return jax.random.normal(k, shape, jnp.float32) * fan_in ** -0.5

    def gain(k, shape):
        return 1.0 + 0.05 * jax.random.normal(k, shape, jnp.float32)

    return {
        "x": jax.random.normal(ks[0], (BATCH, SEQ, D), jnp.float32),
        "ffn1_norm_g": gain(ks[1], (L, D)),
        "ffn1_w_gu": nrm(ks[2], (L, D, 2 * D_FF), D),
        "ffn1_w_down": nrm(ks[3], (L, D_FF, D), D_FF),
        "mix_norm_g": gain(ks[4], (L, D)),
        "w_in": nrm(ks[5], (L, D, IN_COLS), D),
        "b_f": FGATE_BIAS_CENTER + 0.1 * jax.random.normal(ks[6], (L, N_FOX_HEADS), jnp.float32),
        "fox_q_norm_g": gain(ks[7], (L, HEAD_DIM)),
        "fox_k_norm_g": gain(ks[8], (L, HEAD_DIM)),
        "moba_q_norm_g": gain(ks[9], (L, HEAD_DIM)),
        "moba_k_norm_g": gain(ks[10], (L, HEAD_DIM)),
        "w_out": nrm(ks[11], (L, MIX_WIDTH, D), MIX_WIDTH),
        "ffn2_norm_g": gain(ks[12], (L, D)),
        "ffn2_w_gu": nrm(ks[13], (L, D, 2 * D_FF), D),
        "ffn2_w_down": nrm(ks[14], (L, D_FF, D), D_FF),
    }


def reference(x, ffn1_norm_g, ffn1_w_gu, ffn1_w_down, mix_norm_g, w_in, b_f,
              fox_q_norm_g, fox_k_norm_g, moba_q_norm_g, moba_k_norm_g, w_out,
              ffn2_norm_g, ffn2_w_gu, ffn2_w_down):
    slopes = 2.0 ** (-8.0 * jnp.arange(1, N_MOBA_HEADS + 1, dtype=jnp.float32) / N_MOBA_HEADS)
    offsets = [int(o) for o in np.cumsum(IN_SECTIONS)[:-1]]
    for l in range(DEPTH):
        x = x + 0.5 * swiglu_ffn(x, ffn1_norm_g[l], ffn1_w_gu[l], ffn1_w_down[l])
        h = rms_norm(x, mix_norm_g[l])
        proj = h @ w_in[l]
        fq, fk, fv, f_logit, mq, mk, mv = jnp.split(proj, offsets, axis=-1)
        fq = rms_norm(split_heads(fq, N_FOX_HEADS), fox_q_norm_g[l])
        fk = rms_norm(split_heads(fk, N_FOX_HEADS), fox_k_norm_g[l])
        fv = split_heads(fv, N_FOX_HEADS)
        log_f = jax.nn.log_sigmoid((f_logit + b_f[l]).astype(jnp.float32)).transpose(0, 2, 1)
        fox_out = forgetting_attention(fq, fk, fv, log_f)
        mq = rms_norm(split_heads(mq, N_MOBA_HEADS), moba_q_norm_g[l])
        mk = rms_norm(split_heads(mk, N_MOBA_HEADS), moba_k_norm_g[l])
        mv = split_heads(mv, N_MOBA_HEADS)
        moba_out = moba_attention(mq, mk, mv, slopes)
        mixed = jnp.concatenate([merge_heads(fox_out), merge_heads(moba_out)], axis=-1)
        x = x + mixed @ w_out[l]
        x = x + 0.5 * swiglu_ffn(x, ffn2_norm_g[l], ffn2_w_gu[l], ffn2_w_down[l])
    return x
```

```python
import functools

import numpy as np
import ml_dtypes
import jax
import jax.numpy as jnp
from jax import lax
from jax.experimental import pallas as pl
from jax.experimental.pallas import tpu as pltpu

D_MODEL = 1024
HEAD_DIM = 64
N_FOX = 8
N_MOBA = 8
GROUP_W = 512
D_FF = 2816
MOBA_BLOCK = 256
MOBA_TOPK = 3
RMS_EPS = 1e-6
NEG = -1e30

LANES = 128
FF_CHUNK = 256
N_FF_CHUNKS = D_FF // FF_CHUNK
ATT_T = 256
HEAD_AUG = 16
FFN_TM = 1024
PROJ_TM = 512
VMEM_LIMIT = 56 * 1024 * 1024

_BF16 = jnp.bfloat16
_F32 = jnp.float32


def _resident(shape):
    nd = len(shape)
    return pl.BlockSpec(shape, lambda *_: (0,) * nd, pipeline_mode=pl.Buffered(1))


def _rms(x, g):
    return x * lax.rsqrt(jnp.mean(x * x, axis=-1, keepdims=True) + RMS_EPS) * g


def _split3(x):
    hi = x.astype(_BF16)
    r1 = x - hi.astype(_F32)
    mid = r1.astype(_BF16)
    lo = (r1 - mid.astype(_F32)).astype(_BF16)
    return hi, mid, lo


def _ffn_kernel(*refs, with_out_proj):
    if with_out_proj:
        x_ref, fo_ref, mo_ref, wout_ref, g_ref, wgu_ref, wd_ref, o_ref, h_ref, acc_ref = refs
        mixed = jnp.concatenate([fo_ref[...], mo_ref[...]], axis=1)
        x = x_ref[...] + jnp.dot(mixed, wout_ref[...], preferred_element_type=_F32)
    else:
        x_ref, g_ref, wgu_ref, wd_ref, o_ref, h_ref, acc_ref = refs
        x = x_ref[...]
    o_ref[...] = x
    h_ref[...] = _rms(x, g_ref[...]).astype(_BF16)
    acc_ref[...] = jnp.zeros_like(acc_ref)

    def body(c, carry):
        gu = jnp.dot(h_ref[...], wgu_ref[c], preferred_element_type=_F32)
        gate, up = gu[:, :FF_CHUNK], gu[:, FF_CHUNK:]
        act = (gate * jax.nn.sigmoid(gate) * up).astype(_BF16)
        acc_ref[...] += jnp.dot(act, wd_ref[c], preferred_element_type=_F32)
        return carry

    lax.fori_loop(0, N_FF_CHUNKS, body, 0)
    o_ref[...] = o_ref[...] + 0.5 * acc_ref[...]


def _ffn(x2, g, wgu, wd, out_proj=None):
    n = x2.shape[0]
    tm = FFN_TM
    tok = lambda w: pl.BlockSpec((tm, w), lambda i: (i, 0))
    in_specs = [tok(D_MODEL)]
    args = [x2]
    if out_proj is not None:
        fo, mo, wout = out_proj
        in_specs += [tok(GROUP_W), tok(GROUP_W), _resident(wout.shape)]
        args += [fo, mo, wout]
    in_specs += [_resident(g.shape), _resident(wgu.shape), _resident(wd.shape)]
    args += [g, wgu, wd]
    return pl.pallas_call(
        functools.partial(_ffn_kernel, with_out_proj=out_proj is not None),
        grid=(n // tm,),
        in_specs=in_specs,
        out_specs=tok(D_MODEL),
        out_shape=jax.ShapeDtypeStruct((n, D_MODEL), _F32),
        scratch_shapes=[pltpu.VMEM((tm, D_MODEL), _BF16), pltpu.VMEM((tm, D_MODEL), _F32)],
        compiler_params=pltpu.CompilerParams(
            dimension_semantics=("arbitrary",), vmem_limit_bytes=VMEM_LIMIT),
        name="ffn_out" if out_proj is not None else "ffn",
    )(*args)


def _proj_kernel(x_ref, g_ref, w_ref, gfq_ref, gfk_ref, gmq_ref, gmk_ref, bf_ref, pool_ref,
                 tri_ref, e_ref, onesq_ref, onesk_ref,
                 fq_ref, fk_ref, fvt_ref, mq_ref, mk_ref, mvt_ref, faq_ref, fak_ref,
                 carry_ref):
    @pl.when(pl.program_id(1) == 0)
    def _():
        carry_ref[...] = jnp.zeros_like(carry_ref)

    h = _rms(x_ref[0], g_ref[...]).astype(_BF16)
    proj = jnp.dot(h, w_ref[...], preferred_element_type=_F32)

    def head_norm(t, gain_ref):
        ms = jnp.dot((t * t).astype(_BF16), pool_ref[...], preferred_element_type=_F32)
        return (t * lax.rsqrt(ms + RMS_EPS) * gain_ref[...]).astype(_BF16)

    w = GROUP_W
    fq_ref[0] = head_norm(proj[:, 0 * w:1 * w], gfq_ref)
    fk_ref[0] = head_norm(proj[:, 1 * w:2 * w], gfk_ref)
    fvt_ref[0] = proj[:, 2 * w:3 * w].T.astype(_BF16)
    mq_ref[0] = head_norm(proj[:, 3 * w:4 * w], gmq_ref)
    mk_ref[0] = head_norm(proj[:, 4 * w:5 * w], gmk_ref)
    mvt_ref[0] = proj[:, 5 * w:6 * w].T.astype(_BF16)

    z = proj[:, 6 * w:] + bf_ref[...]
    logf = jnp.minimum(z, 0.0) - jnp.log(1.0 + jnp.exp(-jnp.abs(z)))
    lane = lax.broadcasted_iota(jnp.int32, logf.shape, 1)
    logf = jnp.where(lane < N_FOX, logf, 0.0)
    pieces = jnp.concatenate(_split3(logf), axis=1)
    part = jnp.dot(tri_ref[...], pieces, preferred_element_type=_F32)
    cum = part[:, :LANES] + part[:, LANES:2 * LANES] + part[:, 2 * LANES:] + carry_ref[...]
    carry_ref[...] = cum[-1:, :]
    cpieces = jnp.concatenate(_split3(cum), axis=1)
    aug = jnp.dot(cpieces, e_ref[...], preferred_element_type=_F32)
    faq_ref[0] = (aug[:, :w] + onesq_ref[...]).astype(_BF16)
    fak_ref[0] = (aug[:, w:] + onesk_ref[...]).astype(_BF16)


def _proj(x3, g, w_all, gains, bf, consts):
    b, s, _ = x3.shape
    tm = PROJ_TM
    pool, tri, e, onesq, onesk = consts
    row = pl.BlockSpec((1, tm, GROUP_W), lambda bi, i: (bi, i, 0))
    col = pl.BlockSpec((1, GROUP_W, tm), lambda bi, i: (bi, 0, i))
    small = [g, w_all, *gains, bf, pool, tri, e, onesq, onesk]
    return pl.pallas_call(
        _proj_kernel,
        grid=(b, s // tm),
        in_specs=[pl.BlockSpec((1, tm, D_MODEL), lambda bi, i: (bi, i, 0))]
        + [_resident(a.shape) for a in small],
        out_specs=[row, row, col, row, row, col, row, row],
        out_shape=[jax.ShapeDtypeStruct((b, s, GROUP_W), _BF16),
                   jax.ShapeDtypeStruct((b, s, GROUP_W), _BF16),
                   jax.ShapeDtypeStruct((b, GROUP_W, s), _BF16),
                   jax.ShapeDtypeStruct((b, s, GROUP_W), _BF16),
                   jax.ShapeDtypeStruct((b, s, GROUP_W), _BF16),
                   jax.ShapeDtypeStruct((b, GROUP_W, s), _BF16),
                   jax.ShapeDtypeStruct((b, s, GROUP_W), _BF16),
                   jax.ShapeDtypeStruct((b, s, GROUP_W), _BF16)],
        scratch_shapes=[pltpu.VMEM((1, LANES), _F32)],
        compiler_params=pltpu.CompilerParams(
            dimension_semantics=("arbitrary", "arbitrary"), vmem_limit_bytes=VMEM_LIMIT),
        name="proj",
    )(x3, *small)


def _dot_nt(a, b):
    return lax.dot_general(a, b, (((1,), (1,)), ((), ())), preferred_element_type=_F32)


def _attn_kernel(q_ref, k_ref, vt_ref, aq_ref, ak_ref, o_ref, *scratch, moba, batched_aug):
    t = ATT_T
    qi = pl.program_id(2)
    q = q_ref[0]
    aq = aq_ref[0] if batched_aug else aq_ref[...]
    lane = lax.broadcasted_iota(jnp.int32, (1, LANES), 1)

    def k_tile(j):
        rows = pl.ds(pl.multiple_of(j * t, t), t)
        ak = ak_ref[0, rows, :] if batched_aug else ak_ref[rows, :]
        return jnp.concatenate([k_ref[0, rows, :], ak], axis=1)

    def vt_tile(j):
        return vt_ref[0, :, pl.ds(pl.multiple_of(j * t, t), t)]

    if moba:
        kmean_ref, selb_ref = scratch

        @pl.when(qi == 0)
        def _():
            for n in range(kmean_ref.shape[0]):
                blk = k_ref[0, n * MOBA_BLOCK:(n + 1) * MOBA_BLOCK, :].astype(_F32)
                kmean_ref[n:n + 1, :] = jnp.mean(blk, axis=0, keepdims=True)

    outs = []
    for hh in range(2):
        in_head = (lane >= hh * HEAD_DIM) & (lane < (hh + 1) * HEAD_DIM)
        in_aug = (lane >= hh * HEAD_AUG) & (lane < (hh + 1) * HEAD_AUG)
        qh = jnp.where(in_head, q, jnp.zeros_like(q))
        qcat = jnp.concatenate([qh, jnp.where(in_aug, aq, jnp.zeros_like(aq))], axis=1)

        if moba:
            nb = kmean_ref.shape[0]
            km = jnp.concatenate(_split3(kmean_ref[...]), axis=0)
            g3 = _dot_nt(km, qh)
            gate = g3[:nb] + g3[nb:2 * nb] + g3[2 * nb:]
            blk_id = lax.broadcasted_iota(jnp.int32, gate.shape, 0)
            beaten = jnp.zeros(gate.shape, jnp.int32)
            for m in range(nb):
                gm = gate[m:m + 1, :]
                wins = (gm > gate) | ((gm == gate) & (blk_id > m))
                beaten = beaten + jnp.where(wins & (qi > m), 1, 0)
            keep = (beaten < MOBA_TOPK) & (blk_id < qi)
            selb_ref[hh] = jnp.where(keep, 0.0, NEG)

        s = _dot_nt(k_tile(qi), qcat)
        key_pos = lax.broadcasted_iota(jnp.int32, s.shape, 0)
        qry_pos = lax.broadcasted_iota(jnp.int32, s.shape, 1)
        s = jnp.where(key_pos <= qry_pos, s, NEG)
        m0 = jnp.max(s, axis=0, keepdims=True)
        p = jnp.exp(s - m0)
        l0 = jnp.sum(p, axis=0, keepdims=True)
        acc0 = jnp.dot(vt_tile(qi), p.astype(_BF16), preferred_element_type=_F32)

        def body(j, carry):
            m_prev, l_prev, acc_prev = carry
            s = _dot_nt(k_tile(j), qcat)
            if moba:
                s = s + selb_ref[hh, pl.ds(j, 1), :]
            m_new = jnp.maximum(m_prev, jnp.max(s, axis=0, keepdims=True))
            alpha = jnp.exp(m_prev - m_new)
            p = jnp.exp(s - m_new)
            l_new = alpha * l_prev + jnp.sum(p, axis=0, keepdims=True)
            acc_new = alpha * acc_prev + jnp.dot(vt_tile(j), p.astype(_BF16),
                                                 preferred_element_type=_F32)
            return m_new, l_new, acc_new

        _, l_fin, acc_fin = lax.fori_loop(0, qi, body, (m0, l0, acc0))
        outs.append(acc_fin / l_fin)

    row = lax.broadcasted_iota(jnp.int32, outs[0].shape, 0)
    out_t = jnp.where(row < HEAD_DIM, outs[0], outs[1])
    o_ref[0] = out_t.T.astype(_BF16)


def _attention(q, k, vt, aq, ak, moba):
    b, s, _ = q.shape
    t = ATT_T
    npairs = GROUP_W // LANES
    batched_aug = aq.ndim == 3
    if batched_aug:
        aq_spec = pl.BlockSpec((1, t, LANES), lambda bi, p, i: (bi, i, p))
        ak_spec = pl.BlockSpec((1, s, LANES), lambda bi, p, i: (bi, 0, p))
    else:
        aq_spec = pl.BlockSpec((t, LANES), lambda bi, p, i: (i, p))
        ak_spec = pl.BlockSpec((s, LANES), lambda bi, p, i: (0, p))
    scratch = []
    if moba:
        scratch = [pltpu.VMEM((s // MOBA_BLOCK, LANES), _F32),
                   pltpu.VMEM((2, s // MOBA_BLOCK, t), _F32)]
    return pl.pallas_call(
        functools.partial(_attn_kernel, moba=moba, batched_aug=batched_aug),
        grid=(b, npairs, s // t),
        in_specs=[pl.BlockSpec((1, t, LANES), lambda bi, p, i: (bi, i, p)),
                  pl.BlockSpec((1, s, LANES), lambda bi, p, i: (bi, 0, p)),
                  pl.BlockSpec((1, LANES, s), lambda bi, p, i: (bi, p, 0)),
                  aq_spec, ak_spec],
        out_specs=pl.BlockSpec((1, t, LANES), lambda bi, p, i: (bi, i, p)),
        out_shape=jax.ShapeDtypeStruct((b, s, GROUP_W), _BF16),
        scratch_shapes=scratch,
        compiler_params=pltpu.CompilerParams(
            dimension_semantics=("arbitrary", "arbitrary", "arbitrary"),
            vmem_limit_bytes=VMEM_LIMIT),
        name="moba_attn" if moba else "fox_attn",
    )(q, k, vt, aq, ak)


def _aug_base(h):
    return (h // 2) * LANES + (h % 2) * HEAD_AUG


def _np_split3(x):
    x = np.asarray(x, np.float32)
    hi = x.astype(ml_dtypes.bfloat16)
    r1 = x - hi.astype(np.float32)
    mid = r1.astype(ml_dtypes.bfloat16)
    lo = (r1 - mid.astype(np.float32)).astype(ml_dtypes.bfloat16)
    return hi, mid, lo


def _constants(seq):
    idx = np.arange(GROUP_W)
    pool = (idx[:, None] // HEAD_DIM == idx[None, :] // HEAD_DIM).astype(np.float32) / HEAD_DIM
    r = np.arange(PROJ_TM)
    tri = (r[None, :] <= r[:, None]).astype(np.float32)
    e = np.zeros((3 * LANES, 2 * GROUP_W), np.float32)
    onesq = np.zeros((1, GROUP_W), np.float32)
    onesk = np.zeros((1, GROUP_W), np.float32)
    for h in range(N_FOX):
        base = _aug_base(h)
        for piece in range(3):
            e[piece * LANES + h, base + piece] = 1.0
            e[piece * LANES + h, GROUP_W + base + 3 + piece] = -1.0
            onesq[0, base + 3 + piece] = 1.0
            onesk[0, base + piece] = 1.0
    pos = np.arange(seq, dtype=np.float32)
    taq = np.zeros((seq, GROUP_W), ml_dtypes.bfloat16)
    tak = np.zeros((seq, GROUP_W), ml_dtypes.bfloat16)
    for h in range(N_MOBA):
        base = _aug_base(h)
        slope = np.float32(2.0) ** np.float32(-8.0 * (h + 1) / N_MOBA)
        for piece, (qv, kv) in enumerate(zip(_np_split3(-slope * pos), _np_split3(slope * pos))):
            taq[:, base + piece] = qv
            tak[:, base + 3 + piece] = kv
            taq[:, base + 3 + piece] = 1.0
            tak[:, base + piece] = 1.0
    bf = lambda a: jnp.asarray(a, _BF16)
    return ((bf(pool), bf(tri), bf(e), jnp.asarray(onesq), jnp.asarray(onesk)),
            (jnp.asarray(taq), jnp.asarray(tak)))


def _prep_ffn(w_gu, w_down):
    gate = w_gu[:, :D_FF].reshape(D_MODEL, N_FF_CHUNKS, FF_CHUNK)
    up = w_gu[:, D_FF:].reshape(D_MODEL, N_FF_CHUNKS, FF_CHUNK)
    wgu = jnp.concatenate([gate, up], axis=2).transpose(1, 0, 2).astype(_BF16)
    wd = w_down.reshape(N_FF_CHUNKS, FF_CHUNK, D_MODEL).astype(_BF16)
    return wgu, wd


def _prep_w_in(w_in):
    w = GROUP_W
    o = 3 * w + N_FOX
    cols = [w_in[:, :3 * w], w_in[:, o:o + 3 * w], w_in[:, 3 * w:o],
            jnp.zeros((D_MODEL, LANES - N_FOX), w_in.dtype)]
    return jnp.concatenate(cols, axis=1).astype(_BF16)


def kernel(x, ffn1_norm_g, ffn1_w_gu, ffn1_w_down, mix_norm_g, w_in, b_f, fox_q_norm_g,
           fox_k_norm_g, moba_q_norm_g, moba_k_norm_g, w_out, ffn2_norm_g, ffn2_w_gu,
           ffn2_w_down):
    b, s, d = x.shape
    depth = w_in.shape[0]
    assert d == D_MODEL and s % PROJ_TM == 0 and (b * s) % FFN_TM == 0
    proj_consts, (taq, tak) = _constants(s)
    scale = HEAD_DIM ** -0.5
    tile = lambda g: jnp.tile(g, GROUP_W // HEAD_DIM)[None, :]

    x2 = x.reshape(b * s, d)
    for l in range(depth):
        wgu1, wd1 = _prep_ffn(ffn1_w_gu[l], ffn1_w_down[l])
        wgu2, wd2 = _prep_ffn(ffn2_w_gu[l], ffn2_w_down[l])
        x2 = _ffn(x2, ffn1_norm_g[l][None, :], wgu1, wd1)
        gains = [tile(fox_q_norm_g[l]) * scale, tile(fox_k_norm_g[l]),
                 tile(moba_q_norm_g[l]) * scale, tile(moba_k_norm_g[l])]
        bf = jnp.pad(b_f[l], (0, LANES - N_FOX))[None, :]
        fq, fk, fvt, mq, mk, mvt, faq, fak = _proj(
            x2.reshape(b, s, d), mix_norm_g[l][None, :], _prep_w_in(w_in[l]), gains, bf,
            proj_consts)
        fo = _attention(fq, fk, fvt, faq, fak, moba=False)
        mo = _attention(mq, mk, mvt, taq, tak, moba=True)
        x2 = _ffn(x2, ffn2_norm_g[l][None, :], wgu2, wd2,
                  out_proj=(fo.reshape(b * s, GROUP_W), mo.reshape(b * s, GROUP_W),
                            w_out[l].astype(_BF16)))
    return x2.reshape(b, s, d)
```

```python
import functools

import numpy as np
import ml_dtypes
import jax
import jax.numpy as jnp
from jax import lax
from jax.experimental import pallas as pl
from jax.experimental.pallas import tpu as pltpu

D_MODEL = 1024
HEAD_DIM = 64
N_FOX = 8
N_MOBA = 8
GROUP_W = 512
D_FF = 2816
MOBA_BLOCK = 256
MOBA_TOPK = 3
RMS_EPS = 1e-6
NEG = -1e30
LOG2E = 1.4426950408889634

LANES = 128
FF_CHUNK = 256
N_FF_CHUNKS = D_FF // FF_CHUNK
ATT_T = 256
HEAD_AUG = 16
SUM_ROWS = 16
FFN_TM = 1024
PROJ_TM = 512
VMEM_LIMIT = 56 * 1024 * 1024

_BF16 = jnp.bfloat16
_F32 = jnp.float32


def _resident(shape):
    nd = len(shape)
    return pl.BlockSpec(shape, lambda *_: (0,) * nd, pipeline_mode=pl.Buffered(1))


def _rms(x, g):
    return x * lax.rsqrt(jnp.mean(x * x, axis=-1, keepdims=True) + RMS_EPS) * g


def _split3(x):
    hi = x.astype(_BF16)
    r1 = x - hi.astype(_F32)
    mid = r1.astype(_BF16)
    lo = (r1 - mid.astype(_F32)).astype(_BF16)
    return hi, mid, lo


def _ffn_kernel(*refs, with_out_proj):
    if with_out_proj:
        x_ref, fo_ref, mo_ref, wout_ref, g_ref, wgu_ref, wd_ref, o_ref, h_ref, acc_ref = refs
        mixed = jnp.concatenate([fo_ref[...], mo_ref[...]], axis=1)
        x = x_ref[...] + jnp.dot(mixed, wout_ref[...], preferred_element_type=_F32)
        o_ref[...] = x
        res_ref = o_ref
    else:
        x_ref, g_ref, wgu_ref, wd_ref, o_ref, h_ref, acc_ref = refs
        x = x_ref[...]
        res_ref = x_ref
    h_ref[...] = _rms(x, g_ref[...]).astype(_BF16)

    def chunk(c):
        gu = jnp.dot(h_ref[...], wgu_ref[c], preferred_element_type=_F32)
        gate, up = gu[:, :FF_CHUNK], gu[:, FF_CHUNK:]
        act = (gate * jax.nn.sigmoid(gate) * up).astype(_BF16)
        return jnp.dot(act, wd_ref[c], preferred_element_type=_F32)

    def body(c, carry):
        acc_ref[...] += chunk(c)
        return carry

    acc_ref[...] = chunk(0)
    lax.fori_loop(1, N_FF_CHUNKS, body, 0)
    o_ref[...] = res_ref[...] + 0.5 * acc_ref[...]


def _ffn(x2, g, wgu, wd, out_proj=None):
    n = x2.shape[0]
    tm = FFN_TM
    tok = lambda w: pl.BlockSpec((tm, w), lambda i: (i, 0))
    in_specs = [tok(D_MODEL)]
    args = [x2]
    if out_proj is not None:
        fo, mo, wout = out_proj
        in_specs += [tok(GROUP_W), tok(GROUP_W), _resident(wout.shape)]
        args += [fo, mo, wout]
    in_specs += [_resident(g.shape), _resident(wgu.shape), _resident(wd.shape)]
    args += [g, wgu, wd]
    return pl.pallas_call(
        functools.partial(_ffn_kernel, with_out_proj=out_proj is not None),
        grid=(n // tm,),
        in_specs=in_specs,
        out_specs=tok(D_MODEL),
        out_shape=jax.ShapeDtypeStruct((n, D_MODEL), _F32),
        scratch_shapes=[pltpu.VMEM((tm, D_MODEL), _BF16), pltpu.VMEM((tm, D_MODEL), _F32)],
        compiler_params=pltpu.CompilerParams(
            dimension_semantics=("arbitrary",), vmem_limit_bytes=VMEM_LIMIT),
        name="ffn_out" if out_proj is not None else "ffn",
    )(*args)


def _proj_kernel(x_ref, g_ref, w_ref, gfq_ref, gfk_ref, gmq_ref, gmk_ref, bf_ref, pool_ref,
                 tri_ref, e_ref, onesq_ref, onesk_ref,
                 fq_ref, fk_ref, fvt_ref, mq_ref, mk_ref, mvt_ref, faq_ref, fak_ref,
                 carry_ref):
    @pl.when(pl.program_id(1) == 0)
    def _():
        carry_ref[...] = jnp.zeros_like(carry_ref)

    h = _rms(x_ref[0], g_ref[...]).astype(_BF16)
    proj = jnp.dot(h, w_ref[...], preferred_element_type=_F32)

    def head_norm(t, gain_ref):
        ms = jnp.dot((t * t).astype(_BF16), pool_ref[...], preferred_element_type=_F32)
        return (t * lax.rsqrt(ms + RMS_EPS) * gain_ref[...]).astype(_BF16)

    w = GROUP_W
    fq_ref[0] = head_norm(proj[:, 0 * w:1 * w], gfq_ref)
    fk_ref[0] = head_norm(proj[:, 1 * w:2 * w], gfk_ref)
    fvt_ref[0] = proj[:, 2 * w:3 * w].T.astype(_BF16)
    mq_ref[0] = head_norm(proj[:, 3 * w:4 * w], gmq_ref)
    mk_ref[0] = head_norm(proj[:, 4 * w:5 * w], gmk_ref)
    mvt_ref[0] = proj[:, 5 * w:6 * w].T.astype(_BF16)

    z = proj[:, 6 * w:] + bf_ref[...]
    logf = jnp.minimum(z, 0.0) - jnp.log(1.0 + jnp.exp(-jnp.abs(z)))
    lane = lax.broadcasted_iota(jnp.int32, logf.shape, 1)
    logf = jnp.where(lane < N_FOX, logf, 0.0)
    pieces = jnp.concatenate(_split3(logf), axis=1)
    part = jnp.dot(tri_ref[...], pieces, preferred_element_type=_F32)
    cum = part[:, :LANES] + part[:, LANES:2 * LANES] + part[:, 2 * LANES:] + carry_ref[...]
    carry_ref[...] = cum[-1:, :]
    cpieces = jnp.concatenate(_split3(cum * LOG2E), axis=1)
    aug = jnp.dot(cpieces, e_ref[...], preferred_element_type=_F32)
    faq_ref[0] = (aug[:, :w] + onesq_ref[...]).astype(_BF16)
    fak_ref[0] = (aug[:, w:] + onesk_ref[...]).astype(_BF16)


def _proj(x3, g, w_all, gains, bf, consts):
    b, s, _ = x3.shape
    tm = PROJ_TM
    pool, tri, e, onesq, onesk = consts
    row = pl.BlockSpec((1, tm, GROUP_W), lambda bi, i: (bi, i, 0))
    col = pl.BlockSpec((1, GROUP_W, tm), lambda bi, i: (bi, 0, i))
    small = [g, w_all, *gains, bf, pool, tri, e, onesq, onesk]
    return pl.pallas_call(
        _proj_kernel,
        grid=(b, s // tm),
        in_specs=[pl.BlockSpec((1, tm, D_MODEL), lambda bi, i: (bi, i, 0))]
        + [_resident(a.shape) for a in small],
        out_specs=[row, row, col, row, row, col, row, row],
        out_shape=[jax.ShapeDtypeStruct((b, s, GROUP_W), _BF16),
                   jax.ShapeDtypeStruct((b, s, GROUP_W), _BF16),
                   jax.ShapeDtypeStruct((b, GROUP_W, s), _BF16),
                   jax.ShapeDtypeStruct((b, s, GROUP_W), _BF16),
                   jax.ShapeDtypeStruct((b, s, GROUP_W), _BF16),
                   jax.ShapeDtypeStruct((b, GROUP_W, s), _BF16),
                   jax.ShapeDtypeStruct((b, s, GROUP_W), _BF16),
                   jax.ShapeDtypeStruct((b, s, GROUP_W), _BF16)],
        scratch_shapes=[pltpu.VMEM((1, LANES), _F32)],
        compiler_params=pltpu.CompilerParams(
            dimension_semantics=("arbitrary", "arbitrary"), vmem_limit_bytes=VMEM_LIMIT),
        name="proj",
    )(x3, *small)


def _dot_nt(a, b):
    return lax.dot_general(a, b, (((1,), (1,)), ((), ())), preferred_element_type=_F32)


def _attn_kernel(q_ref, k_ref, vt_ref, aq_ref, ak_ref, o_ref, qcat_ref, s_ref, acc_ref,
                 *scratch, moba, batched_aug):
    t = ATT_T
    seq = q_ref.shape[1]
    n_super = seq // (2 * t)
    lane = lax.broadcasted_iota(jnp.int32, (1, LANES), 1)
    ones_rows = jnp.ones((SUM_ROWS, 2 * t), _BF16)
    chains = [(half, hh) for half in range(2) for hh in range(2)]

    def k_rows(start, n):
        ak = ak_ref[0, start:start + n, :] if batched_aug else ak_ref[start:start + n, :]
        return jnp.concatenate([k_ref[0, start:start + n, :], ak], axis=1)

    def vt_cols(hh, start, n):
        vt = vt_ref[0, hh * HEAD_DIM:(hh + 1) * HEAD_DIM, start:start + n]
        return jnp.concatenate([vt, ones_rows[:, :n]], axis=0)

    q = q_ref[0]
    aq = aq_ref[0] if batched_aug else aq_ref[...]
    if moba:
        kmean_ref, selb_ref = scratch
        nb = kmean_ref.shape[0]
        for n in range(nb):
            blk = k_ref[0, n * MOBA_BLOCK:(n + 1) * MOBA_BLOCK, :].astype(_F32)
            kmean_ref[n:n + 1, :] = jnp.mean(blk, axis=0, keepdims=True)
        km = jnp.concatenate(_split3(kmean_ref[...]), axis=0)

    for hh in range(2):
        in_head = (lane >= hh * HEAD_DIM) & (lane < (hh + 1) * HEAD_DIM)
        in_aug = (lane >= hh * HEAD_AUG) & (lane < (hh + 1) * HEAD_AUG)
        qh = jnp.where(in_head, q, jnp.zeros_like(q))
        qa = jnp.where(in_aug, aq, jnp.zeros_like(aq))
        for sb in range(n_super):
            for half in range(2):
                r0 = (2 * sb + half) * t
                qcat_ref[sb, 2 * half + hh] = jnp.concatenate([qh[r0:r0 + t], qa[r0:r0 + t]],
                                                              axis=1)
        if moba:
            g3 = _dot_nt(km, qh)
            gate = g3[:nb] + g3[nb:2 * nb] + g3[2 * nb:]
            blk_id = lax.broadcasted_iota(jnp.int32, gate.shape, 0)
            n_past = lax.broadcasted_iota(jnp.int32, gate.shape, 1) // MOBA_BLOCK
            beaten = jnp.zeros(gate.shape, jnp.int32)
            for m in range(nb):
                gm = gate[m:m + 1, :]
                wins = (gm > gate) | ((gm == gate) & (blk_id > m))
                beaten = beaten + jnp.where(wins & (n_past > m), 1, 0)
            keep = (beaten < MOBA_TOPK) & (blk_id < n_past)
            selb_ref[hh] = jnp.where(keep, 0.0, NEG)

    def sel_row(hh, q_tile, n):
        return selb_ref[hh, n:n + 1, q_tile * t:(q_tile + 1) * t]

    stages = [(sb, j) for sb in range(n_super) for j in range(sb + 1)]

    def issue_scores(n):
        sb, j = stages[n]
        kk = k_rows(2 * j * t, 2 * t)
        for c, (half, _) in enumerate(chains):
            nk = t if (j == sb and half == 0) else 2 * t
            s_ref[n % 2, c, :nk, :] = _dot_nt(kk[:nk], qcat_ref[sb, c])

    issue_scores(0)
    m_run = [None] * 4
    for n, (sb, j) in enumerate(stages):
        if n + 1 < len(stages):
            issue_scores(n + 1)
        diag = j == sb
        vts = [vt_cols(hh, 2 * j * t, 2 * t) for hh in range(2)]
        for c, (half, hh) in enumerate(chains):
            vv = vts[hh]
            q_tile = 2 * sb + half
            if diag:
                nk = (half + 1) * t
                s = s_ref[n % 2, c, :nk, :]
                key_pos = lax.broadcasted_iota(jnp.int32, s.shape, 0)
                qry_pos = lax.broadcasted_iota(jnp.int32, s.shape, 1) + half * t
                s = jnp.where(key_pos <= qry_pos, s, NEG)
                if moba and half == 1:
                    s = jnp.concatenate([s[:t] + sel_row(hh, q_tile, 2 * j), s[t:]], axis=0)
            else:
                nk = 2 * t
                s = s_ref[n % 2, c]
                if moba:
                    s = jnp.concatenate([s[:t] + sel_row(hh, q_tile, 2 * j),
                                         s[t:] + sel_row(hh, q_tile, 2 * j + 1)], axis=0)
            m_blk = jnp.max(s, axis=0, keepdims=True)
            if j == 0:
                m_new = m_blk
                p = jnp.exp2(s - m_new).astype(_BF16)
                acc_ref[c] = jnp.dot(vv[:, :nk], p, preferred_element_type=_F32)
            else:
                m_new = jnp.maximum(m_run[c], m_blk)
                alpha = jnp.exp2(m_run[c] - m_new)
                p = jnp.exp2(s - m_new).astype(_BF16)
                acc_ref[c] = alpha * acc_ref[c] + jnp.dot(vv[:, :nk], p,
                                                          preferred_element_type=_F32)
            m_run[c] = m_new
        if diag:
            for half in range(2):
                a0 = acc_ref[2 * half]
                a1 = acc_ref[2 * half + 1]
                out_t = jnp.concatenate([a0[:HEAD_DIM] / a0[HEAD_DIM:HEAD_DIM + 1],
                                         a1[:HEAD_DIM] / a1[HEAD_DIM:HEAD_DIM + 1]],
                                        axis=0)
                r0 = (2 * sb + half) * t
                o_ref[0, r0:r0 + t, :] = out_t.T.astype(_BF16)


def _attention(q, k, vt, aq, ak, moba):
    b, s, _ = q.shape
    t = ATT_T
    npairs = GROUP_W // LANES
    batched_aug = aq.ndim == 3
    seq_blk = pl.BlockSpec((1, s, LANES), lambda bi, p: (bi, 0, p))
    aug_blk = seq_blk if batched_aug else pl.BlockSpec((s, LANES), lambda bi, p: (0, p))
    scratch = [pltpu.VMEM((s // (2 * t), 4, t, 2 * LANES), _BF16),
               pltpu.VMEM((2, 4, 2 * t, t), _F32),
               pltpu.VMEM((4, HEAD_DIM + SUM_ROWS, t), _F32)]
    if moba:
        scratch += [pltpu.VMEM((s // MOBA_BLOCK, LANES), _F32),
                    pltpu.VMEM((2, s // MOBA_BLOCK, s), _F32)]
    return pl.pallas_call(
        functools.partial(_attn_kernel, moba=moba, batched_aug=batched_aug),
        grid=(b, npairs),
        in_specs=[seq_blk, seq_blk,
                  pl.BlockSpec((1, LANES, s), lambda bi, p: (bi, p, 0)),
                  aug_blk, aug_blk],
        out_specs=seq_blk,
        out_shape=jax.ShapeDtypeStruct((b, s, GROUP_W), _BF16),
        scratch_shapes=scratch,
        compiler_params=pltpu.CompilerParams(
            dimension_semantics=("arbitrary", "arbitrary"), vmem_limit_bytes=VMEM_LIMIT),
        name="moba_attn" if moba else "fox_attn",
    )(q, k, vt, aq, ak)


def _aug_base(h):
    return (h // 2) * LANES + (h % 2) * HEAD_AUG


def _np_split3(x):
    x = np.asarray(x, np.float32)
    hi = x.astype(ml_dtypes.bfloat16)
    r1 = x - hi.astype(np.float32)
    mid = r1.astype(ml_dtypes.bfloat16)
    lo = (r1 - mid.astype(np.float32)).astype(ml_dtypes.bfloat16)
    return hi, mid, lo


def _constants(seq):
    idx = np.arange(GROUP_W)
    pool = (idx[:, None] // HEAD_DIM == idx[None, :] // HEAD_DIM).astype(np.float32) / HEAD_DIM
    r = np.arange(PROJ_TM)
    tri = (r[None, :] <= r[:, None]).astype(np.float32)
    e = np.zeros((3 * LANES, 2 * GROUP_W), np.float32)
    onesq = np.zeros((1, GROUP_W), np.float32)
    onesk = np.zeros((1, GROUP_W), np.float32)
    for h in range(N_FOX):
        base = _aug_base(h)
        for piece in range(3):
            e[piece * LANES + h, base + piece] = 1.0
            e[piece * LANES + h, GROUP_W + base + 3 + piece] = -1.0
            onesq[0, base + 3 + piece] = 1.0
            onesk[0, base + piece] = 1.0
    pos = np.arange(seq, dtype=np.float32)
    taq = np.zeros((seq, GROUP_W), ml_dtypes.bfloat16)
    tak = np.zeros((seq, GROUP_W), ml_dtypes.bfloat16)
    for h in range(N_MOBA):
        base = _aug_base(h)
        slope = np.float32(2.0) ** np.float32(-8.0 * (h + 1) / N_MOBA)
        ramp = slope * pos * np.float32(LOG2E)
        for piece, (qv, kv) in enumerate(zip(_np_split3(-ramp), _np_split3(ramp))):
            taq[:, base + piece] = qv
            tak[:, base + 3 + piece] = kv
            taq[:, base + 3 + piece] = 1.0
            tak[:, base + piece] = 1.0
    bf = lambda a: jnp.asarray(a, _BF16)
    return ((bf(pool), bf(tri), bf(e), jnp.asarray(onesq), jnp.asarray(onesk)),
            (jnp.asarray(taq), jnp.asarray(tak)))


def _prep_ffn(w_gu, w_down):
    gate = w_gu[:, :D_FF].reshape(D_MODEL, N_FF_CHUNKS, FF_CHUNK)
    up = w_gu[:, D_FF:].reshape(D_MODEL, N_FF_CHUNKS, FF_CHUNK)
    wgu = jnp.concatenate([gate, up], axis=2).transpose(1, 0, 2).astype(_BF16)
    wd = w_down.reshape(N_FF_CHUNKS, FF_CHUNK, D_MODEL).astype(_BF16)
    return wgu, wd


def _prep_w_in(w_in):
    w = GROUP_W
    o = 3 * w + N_FOX
    cols = [w_in[:, :3 * w], w_in[:, o:o + 3 * w], w_in[:, 3 * w:o],
            jnp.zeros((D_MODEL, LANES - N_FOX), w_in.dtype)]
    return jnp.concatenate(cols, axis=1).astype(_BF16)


def kernel(x, ffn1_norm_g, ffn1_w_gu, ffn1_w_down, mix_norm_g, w_in, b_f, fox_q_norm_g,
           fox_k_norm_g, moba_q_norm_g, moba_k_norm_g, w_out, ffn2_norm_g, ffn2_w_gu,
           ffn2_w_down):
    b, s, d = x.shape
    depth = w_in.shape[0]
    assert d == D_MODEL and s % (2 * ATT_T) == 0 and (b * s) % FFN_TM == 0
    proj_consts, (taq, tak) = _constants(s)
    q_scale = np.float32(HEAD_DIM ** -0.5 * LOG2E)
    tile = lambda g: jnp.tile(g, GROUP_W // HEAD_DIM)[None, :]

    x2 = x.reshape(b * s, d)
    for l in range(depth):
        wgu1, wd1 = _prep_ffn(ffn1_w_gu[l], ffn1_w_down[l])
        wgu2, wd2 = _prep_ffn(ffn2_w_gu[l], ffn2_w_down[l])
        x2 = _ffn(x2, ffn1_norm_g[l][None, :], wgu1, wd1)
        gains = [tile(fox_q_norm_g[l]) * q_scale, tile(fox_k_norm_g[l]),
                 tile(moba_q_norm_g[l]) * q_scale, tile(moba_k_norm_g[l])]
        bf = jnp.pad(b_f[l], (0, LANES - N_FOX))[None, :]
        fq, fk, fvt, mq, mk, mvt, faq, fak = _proj(
            x2.reshape(b, s, d), mix_norm_g[l][None, :], _prep_w_in(w_in[l]), gains, bf,
            proj_consts)
        fo = _attention(fq, fk, fvt, faq, fak, moba=False)
        mo = _attention(mq, mk, mvt, taq, tak, moba=True)
        x2 = _ffn(x2, ffn2_norm_g[l][None, :], wgu2, wd2,
                  out_proj=(fo.reshape(b * s, GROUP_W), mo.reshape(b * s, GROUP_W),
                            w_out[l].astype(_BF16)))
    return x2.reshape(b, s, d)
```

```python
import functools

import numpy as np
import ml_dtypes
import jax
import jax.numpy as jnp
from jax import lax
from jax.experimental import pallas as pl
from jax.experimental.pallas import tpu as pltpu

D_MODEL = 1024
HEAD_DIM = 64
N_FOX = 8
N_MOBA = 8
GROUP_W = 512
D_FF = 2816
MOBA_BLOCK = 256
MOBA_TOPK = 3
RMS_EPS = 1e-6
NEG = -1e30
LOG2E = 1.4426950408889634

LANES = 128
FF_CHUNK = 512
FF_HEAD = D_FF % FF_CHUNK
N_FF_CHUNKS = D_FF // FF_CHUNK
ATT_T = 256
HEAD_AUG = 16
SUM_ROWS = 16
FFN_TM = 1024
PROJ_TM = 512
VMEM_LIMIT = 56 * 1024 * 1024

_BF16 = jnp.bfloat16
_F32 = jnp.float32


def _resident(shape):
    nd = len(shape)
    return pl.BlockSpec(shape, lambda *_: (0,) * nd, pipeline_mode=pl.Buffered(1))


def _layer(shape, l):
    nd = len(shape) - 1
    return pl.BlockSpec((None,) + tuple(shape[1:]), lambda *_: (l,) + (0,) * nd,
                        pipeline_mode=pl.Buffered(1))


def _rms(x, g):
    return x * lax.rsqrt(jnp.mean(x * x, axis=-1, keepdims=True) + RMS_EPS) * g


def _split3(x):
    hi = x.astype(_BF16)
    r1 = x - hi.astype(_F32)
    mid = r1.astype(_BF16)
    lo = (r1 - mid.astype(_F32)).astype(_BF16)
    return hi, mid, lo


def _ffn_kernel(*refs, with_out_proj):
    if with_out_proj:
        x_ref, fo_ref, mo_ref, wout_ref, g_ref, wgu_ref, wd_ref, o_ref, h_ref, acc_ref = refs
        mixed = jnp.concatenate([fo_ref[...], mo_ref[...]], axis=1)
        x = x_ref[...] + jnp.dot(mixed, wout_ref[...], preferred_element_type=_F32)
        o_ref[...] = x
        res_ref = o_ref
    else:
        x_ref, g_ref, wgu_ref, wd_ref, o_ref, h_ref, acc_ref = refs
        x = x_ref[...]
        res_ref = x_ref
    h_ref[...] = _rms(x, g_ref[...]).astype(_BF16)

    def chunk(lo, width):
        h = h_ref[...]
        gate = jnp.dot(h, wgu_ref[:, pl.ds(lo, width)], preferred_element_type=_F32)
        up = jnp.dot(h, wgu_ref[:, pl.ds(D_FF + lo, width)], preferred_element_type=_F32)
        act = (gate * jax.nn.sigmoid(gate) * up).astype(_BF16)
        return jnp.dot(act, wd_ref[pl.ds(lo, width), :], preferred_element_type=_F32)

    def body(c, carry):
        lo = pl.multiple_of(FF_HEAD + c * FF_CHUNK, FF_HEAD)
        acc_ref[...] += chunk(lo, FF_CHUNK)
        return carry

    acc_ref[...] = chunk(0, FF_HEAD)
    lax.fori_loop(0, N_FF_CHUNKS, body, 0)
    o_ref[...] = res_ref[...] + 0.5 * acc_ref[...]


def _ffn(x2, l, g, wgu, wd, out_proj=None):
    n = x2.shape[0]
    tm = FFN_TM
    tok = lambda w: pl.BlockSpec((tm, w), lambda i: (i, 0))
    in_specs = [tok(D_MODEL)]
    args = [x2]
    if out_proj is not None:
        fo, mo, wout = out_proj
        in_specs += [tok(GROUP_W), tok(GROUP_W), _layer(wout.shape, l)]
        args += [fo, mo, wout]
    in_specs += [_layer(g.shape, l), _layer(wgu.shape, l), _layer(wd.shape, l)]
    args += [g, wgu, wd]
    return pl.pallas_call(
        functools.partial(_ffn_kernel, with_out_proj=out_proj is not None),
        grid=(n // tm,),
        in_specs=in_specs,
        out_specs=tok(D_MODEL),
        out_shape=jax.ShapeDtypeStruct((n, D_MODEL), _F32),
        scratch_shapes=[pltpu.VMEM((tm, D_MODEL), _BF16), pltpu.VMEM((tm, D_MODEL), _F32)],
        compiler_params=pltpu.CompilerParams(
            dimension_semantics=("arbitrary",), vmem_limit_bytes=VMEM_LIMIT),
        name="ffn_out" if out_proj is not None else "ffn",
    )(*args)


def _proj_kernel(x_ref, g_ref, w_ref, gfq_ref, gfk_ref, gmq_ref, gmk_ref, bf_ref, pool_ref,
                 tri_ref, e_ref, onesq_ref, onesk_ref,
                 fq_ref, fk_ref, fvt_ref, mq_ref, mk_ref, mvt_ref, faq_ref, fak_ref,
                 carry_ref):
    @pl.when(pl.program_id(1) == 0)
    def _():
        carry_ref[...] = jnp.zeros_like(carry_ref)

    h = _rms(x_ref[0], g_ref[...]).astype(_BF16)
    proj = jnp.dot(h, w_ref[...], preferred_element_type=_F32)

    def head_norm(t, gain_ref):
        ms = jnp.dot((t * t).astype(_BF16), pool_ref[...], preferred_element_type=_F32)
        return (t * lax.rsqrt(ms + RMS_EPS) * gain_ref[...]).astype(_BF16)

    w = GROUP_W
    fq_ref[0] = head_norm(proj[:, 0 * w:1 * w], gfq_ref)
    fk_ref[0] = head_norm(proj[:, 1 * w:2 * w], gfk_ref)
    fvt_ref[0] = proj[:, 2 * w:3 * w].T.astype(_BF16)
    mq_ref[0] = head_norm(proj[:, 3 * w:4 * w], gmq_ref)
    mk_ref[0] = head_norm(proj[:, 4 * w:5 * w], gmk_ref)
    mvt_ref[0] = proj[:, 5 * w:6 * w].T.astype(_BF16)

    z = proj[:, 6 * w:] + bf_ref[...]
    logf = jnp.minimum(z, 0.0) - jnp.log(1.0 + jnp.exp(-jnp.abs(z)))
    lane = lax.broadcasted_iota(jnp.int32, logf.shape, 1)
    logf = jnp.where(lane < N_FOX, logf, 0.0)
    pieces = jnp.concatenate(_split3(logf), axis=1)
    part = jnp.dot(tri_ref[...], pieces, preferred_element_type=_F32)
    cum = part[:, :LANES] + part[:, LANES:2 * LANES] + part[:, 2 * LANES:] + carry_ref[...]
    carry_ref[...] = cum[-1:, :]
    cpieces = jnp.concatenate(_split3(cum * LOG2E), axis=1)
    aug = jnp.dot(cpieces, e_ref[...], preferred_element_type=_F32)
    faq_ref[0] = (aug[:, :LANES] + onesq_ref[...]).astype(_BF16)
    fak_ref[0] = (aug[:, LANES:] + onesk_ref[...]).astype(_BF16)


def _proj(x3, l, g, w_all, gains, bf, consts):
    b, s, _ = x3.shape
    tm = PROJ_TM
    row = pl.BlockSpec((1, tm, GROUP_W), lambda bi, i: (bi, i, 0))
    col = pl.BlockSpec((1, GROUP_W, tm), lambda bi, i: (bi, 0, i))
    aug = pl.BlockSpec((1, tm, LANES), lambda bi, i: (bi, i, 0))
    stacked = [g, w_all, *gains, bf]
    small = stacked + list(consts)
    return pl.pallas_call(
        _proj_kernel,
        grid=(b, s // tm),
        in_specs=[pl.BlockSpec((1, tm, D_MODEL), lambda bi, i: (bi, i, 0))]
        + [_layer(a.shape, l) for a in stacked] + [_resident(a.shape) for a in consts],
        out_specs=[row, row, col, row, row, col, aug, aug],
        out_shape=[jax.ShapeDtypeStruct((b, s, GROUP_W), _BF16),
                   jax.ShapeDtypeStruct((b, s, GROUP_W), _BF16),
                   jax.ShapeDtypeStruct((b, GROUP_W, s), _BF16),
                   jax.ShapeDtypeStruct((b, s, GROUP_W), _BF16),
                   jax.ShapeDtypeStruct((b, s, GROUP_W), _BF16),
                   jax.ShapeDtypeStruct((b, GROUP_W, s), _BF16),
                   jax.ShapeDtypeStruct((b, s, LANES), _BF16),
                   jax.ShapeDtypeStruct((b, s, LANES), _BF16)],
        scratch_shapes=[pltpu.VMEM((1, LANES), _F32)],
        compiler_params=pltpu.CompilerParams(
            dimension_semantics=("arbitrary", "arbitrary"), vmem_limit_bytes=VMEM_LIMIT),
        name="proj",
    )(x3, *small)


def _dot_nt(a, b):
    return lax.dot_general(a, b, (((1,), (1,)), ((), ())), preferred_element_type=_F32)


def _attn_kernel(q_ref, k_ref, vt_ref, aq_ref, ak_ref, o_ref, qcat_ref, s_ref, acc_ref,
                 *scratch, moba, batched_aug):
    t = ATT_T
    seq = q_ref.shape[1]
    n_super = seq // (2 * t)
    lane = lax.broadcasted_iota(jnp.int32, (1, LANES), 1)
    ones_rows = jnp.ones((SUM_ROWS, 2 * t), _BF16)
    chains = [(half, hh) for half in range(2) for hh in range(2)]

    def k_rows(start, n):
        ak = ak_ref[0, start:start + n, :] if batched_aug else ak_ref[start:start + n, :]
        return jnp.concatenate([k_ref[0, start:start + n, :], ak], axis=1)

    def vt_cols(hh, start, n):
        vt = vt_ref[0, hh * HEAD_DIM:(hh + 1) * HEAD_DIM, start:start + n]
        return jnp.concatenate([vt, ones_rows[:, :n]], axis=0)

    q = q_ref[0]
    aq = aq_ref[0] if batched_aug else aq_ref[...]
    if moba:
        kmean_ref, selb_ref = scratch
        nb = kmean_ref.shape[0]
        for n in range(nb):
            blk = k_ref[0, n * MOBA_BLOCK:(n + 1) * MOBA_BLOCK, :].astype(_F32)
            kmean_ref[n:n + 1, :] = jnp.mean(blk, axis=0, keepdims=True)
        km = jnp.concatenate(_split3(kmean_ref[...]), axis=0)

    for hh in range(2):
        in_head = (lane >= hh * HEAD_DIM) & (lane < (hh + 1) * HEAD_DIM)
        aug_lo = (2 * pl.program_id(1) + hh) * HEAD_AUG
        in_aug = (lane >= aug_lo) & (lane < aug_lo + HEAD_AUG)
        qh = jnp.where(in_head, q, jnp.zeros_like(q))
        qa = jnp.where(in_aug, aq, jnp.zeros_like(aq))
        for sb in range(n_super):
            for half in range(2):
                r0 = (2 * sb + half) * t
                qcat_ref[sb, 2 * half + hh] = jnp.concatenate([qh[r0:r0 + t], qa[r0:r0 + t]],
                                                              axis=1)
        if moba:
            g3 = _dot_nt(km, qh)
            gate = g3[:nb] + g3[nb:2 * nb] + g3[2 * nb:]
            blk_id = lax.broadcasted_iota(jnp.int32, gate.shape, 0)
            n_past = lax.broadcasted_iota(jnp.int32, gate.shape, 1) // MOBA_BLOCK
            beaten = jnp.zeros(gate.shape, jnp.int32)
            for m in range(nb):
                gm = gate[m:m + 1, :]
                wins = (gm > gate) | ((gm == gate) & (blk_id > m))
                beaten = beaten + jnp.where(wins & (n_past > m), 1, 0)
            keep = (beaten < MOBA_TOPK) & (blk_id < n_past)
            selb_ref[hh] = jnp.where(keep, 0.0, NEG)

    def sel_row(hh, q_tile, n):
        return selb_ref[hh, n:n + 1, q_tile * t:(q_tile + 1) * t]

    stages = [(sb, j) for sb in range(n_super) for j in range(sb + 1)]

    def issue_scores(n, c):
        sb, j = stages[n]
        half = chains[c][0]
        nk = t if (j == sb and half == 0) else 2 * t
        kk = k_rows(2 * j * t, nk)
        s_ref[n % 2, c, :nk, :] = _dot_nt(kk, qcat_ref[sb, c])

    for c in range(4):
        issue_scores(0, c)
    m_run = [None] * 4
    for n, (sb, j) in enumerate(stages):
        diag = j == sb
        vts = [vt_cols(hh, 2 * j * t, 2 * t) for hh in range(2)]
        for c, (half, hh) in enumerate(chains):
            if n + 1 < len(stages):
                issue_scores(n + 1, c)
            vv = vts[hh]
            q_tile = 2 * sb + half
            if diag:
                nk = (half + 1) * t
                s = s_ref[n % 2, c, :nk, :]
                key_pos = lax.broadcasted_iota(jnp.int32, s.shape, 0)
                qry_pos = lax.broadcasted_iota(jnp.int32, s.shape, 1) + half * t
                s = jnp.where(key_pos <= qry_pos, s, NEG)
                if moba and half == 1:
                    s = jnp.concatenate([s[:t] + sel_row(hh, q_tile, 2 * j), s[t:]], axis=0)
            else:
                nk = 2 * t
                s = s_ref[n % 2, c]
                if moba:
                    s = jnp.concatenate([s[:t] + sel_row(hh, q_tile, 2 * j),
                                         s[t:] + sel_row(hh, q_tile, 2 * j + 1)], axis=0)
            m_blk = jnp.max(s, axis=0, keepdims=True)
            if j == 0:
                m_new = m_blk
                p = jnp.exp2(s - m_new).astype(_BF16)
                acc_ref[c] = jnp.dot(vv[:, :nk], p, preferred_element_type=_F32)
            else:
                m_new = jnp.maximum(m_run[c], m_blk)
                alpha = jnp.exp2(m_run[c] - m_new)
                p = jnp.exp2(s - m_new).astype(_BF16)
                acc_ref[c] = alpha * acc_ref[c] + jnp.dot(vv[:, :nk], p,
                                                          preferred_element_type=_F32)
            m_run[c] = m_new
        if diag:
            for half in range(2):
                a0 = acc_ref[2 * half]
                a1 = acc_ref[2 * half + 1]
                out_t = jnp.concatenate([a0[:HEAD_DIM] / a0[HEAD_DIM:HEAD_DIM + 1],
                                         a1[:HEAD_DIM] / a1[HEAD_DIM:HEAD_DIM + 1]],
                                        axis=0)
                r0 = (2 * sb + half) * t
                o_ref[0, r0:r0 + t, :] = out_t.T.astype(_BF16)


def _attention(q, k, vt, aq, ak, moba):
    b, s, _ = q.shape
    t = ATT_T
    npairs = GROUP_W // LANES
    batched_aug = aq.ndim == 3
    seq_blk = pl.BlockSpec((1, s, LANES), lambda bi, p: (bi, 0, p))
    aug_blk = (pl.BlockSpec((1, s, LANES), lambda bi, p: (bi, 0, 0)) if batched_aug
               else pl.BlockSpec((s, LANES), lambda bi, p: (0, 0)))
    scratch = [pltpu.VMEM((s // (2 * t), 4, t, 2 * LANES), _BF16),
               pltpu.VMEM((2, 4, 2 * t, t), _F32),
               pltpu.VMEM((4, HEAD_DIM + SUM_ROWS, t), _F32)]
    if moba:
        scratch += [pltpu.VMEM((s // MOBA_BLOCK, LANES), _F32),
                    pltpu.VMEM((2, s // MOBA_BLOCK, s), _F32)]
    return pl.pallas_call(
        functools.partial(_attn_kernel, moba=moba, batched_aug=batched_aug),
        grid=(b, npairs),
        in_specs=[seq_blk, seq_blk,
                  pl.BlockSpec((1, LANES, s), lambda bi, p: (bi, p, 0)),
                  aug_blk, aug_blk],
        out_specs=seq_blk,
        out_shape=jax.ShapeDtypeStruct((b, s, GROUP_W), _BF16),
        scratch_shapes=scratch,
        compiler_params=pltpu.CompilerParams(
            dimension_semantics=("arbitrary", "arbitrary"), vmem_limit_bytes=VMEM_LIMIT),
        name="moba_attn" if moba else "fox_attn",
    )(q, k, vt, aq, ak)


def _aug_base(h):
    return h * HEAD_AUG


def _np_split3(x):
    x = np.asarray(x, np.float32)
    hi = x.astype(ml_dtypes.bfloat16)
    r1 = x - hi.astype(np.float32)
    mid = r1.astype(ml_dtypes.bfloat16)
    lo = (r1 - mid.astype(np.float32)).astype(ml_dtypes.bfloat16)
    return hi, mid, lo


def _constants(seq):
    idx = np.arange(GROUP_W)
    pool = (idx[:, None] // HEAD_DIM == idx[None, :] // HEAD_DIM).astype(np.float32) / HEAD_DIM
    r = np.arange(PROJ_TM)
    tri = (r[None, :] <= r[:, None]).astype(np.float32)
    e = np.zeros((3 * LANES, 2 * LANES), np.float32)
    onesq = np.zeros((1, LANES), np.float32)
    onesk = np.zeros((1, LANES), np.float32)
    for h in range(N_FOX):
        base = _aug_base(h)
        for piece in range(3):
            e[piece * LANES + h, base + piece] = 1.0
            e[piece * LANES + h, LANES + base + 3 + piece] = -1.0
            onesq[0, base + 3 + piece] = 1.0
            onesk[0, base + piece] = 1.0
    pos = np.arange(seq, dtype=np.float32)
    taq = np.zeros((seq, LANES), ml_dtypes.bfloat16)
    tak = np.zeros((seq, LANES), ml_dtypes.bfloat16)
    for h in range(N_MOBA):
        base = _aug_base(h)
        slope = np.float32(2.0) ** np.float32(-8.0 * (h + 1) / N_MOBA)
        ramp = slope * pos * np.float32(LOG2E)
        for piece, (qv, kv) in enumerate(zip(_np_split3(-ramp), _np_split3(ramp))):
            taq[:, base + piece] = qv
            tak[:, base + 3 + piece] = kv
            taq[:, base + 3 + piece] = 1.0
            tak[:, base + piece] = 1.0
    bf = lambda a: jnp.asarray(a, _BF16)
    return ((bf(pool), bf(tri), bf(e), jnp.asarray(onesq), jnp.asarray(onesk)),
            (jnp.asarray(taq), jnp.asarray(tak)))


def _prep_w_in(w_in):
    w = GROUP_W
    o = 3 * w + N_FOX
    pad = jnp.zeros(w_in.shape[:2] + (LANES - N_FOX,), w_in.dtype)
    cols = [w_in[..., :3 * w], w_in[..., o:o + 3 * w], w_in[..., 3 * w:o], pad]
    return jnp.concatenate(cols, axis=-1).astype(_BF16)


def kernel(x, ffn1_norm_g, ffn1_w_gu, ffn1_w_down, mix_norm_g, w_in, b_f, fox_q_norm_g,
           fox_k_norm_g, moba_q_norm_g, moba_k_norm_g, w_out, ffn2_norm_g, ffn2_w_gu,
           ffn2_w_down):
    b, s, d = x.shape
    depth = w_in.shape[0]
    assert d == D_MODEL and s % (2 * ATT_T) == 0 and (b * s) % FFN_TM == 0
    proj_consts, (taq, tak) = _constants(s)
    q_scale = np.float32(HEAD_DIM ** -0.5 * LOG2E)
    tile = lambda g: jnp.tile(g, (1, GROUP_W // HEAD_DIM))[:, None, :]
    row = lambda g: g[:, None, :]

    ffn1 = (row(ffn1_norm_g), ffn1_w_gu.astype(_BF16), ffn1_w_down.astype(_BF16))
    ffn2 = (row(ffn2_norm_g), ffn2_w_gu.astype(_BF16), ffn2_w_down.astype(_BF16))
    w_out_b = w_out.astype(_BF16)
    w_all = _prep_w_in(w_in)
    gains = [tile(fox_q_norm_g) * q_scale, tile(fox_k_norm_g),
             tile(moba_q_norm_g) * q_scale, tile(moba_k_norm_g)]
    bf = jnp.pad(b_f, ((0, 0), (0, LANES - N_FOX)))[:, None, :]
    mix_g = row(mix_norm_g)

    x2 = x.reshape(b * s, d)
    for l in range(depth):
        x2 = _ffn(x2, l, *ffn1)
        fq, fk, fvt, mq, mk, mvt, faq, fak = _proj(
            x2.reshape(b, s, d), l, mix_g, w_all, gains, bf, proj_consts)
        fo = _attention(fq, fk, fvt, faq, fak, moba=False)
        mo = _attention(mq, mk, mvt, taq, tak, moba=True)
        x2 = _ffn(x2, l, *ffn2,
                  out_proj=(fo.reshape(b * s, GROUP_W), mo.reshape(b * s, GROUP_W), w_out_b))
    return x2.reshape(b, s, d)
```

```python
import functools

import numpy as np
import ml_dtypes
import jax
import jax.numpy as jnp
from jax import lax
from jax.experimental import pallas as pl
from jax.experimental.pallas import tpu as pltpu

D_MODEL = 1024
HEAD_DIM = 64
N_FOX = 8
N_MOBA = 8
GROUP_W = 512
D_FF = 2816
MOBA_BLOCK = 256
MOBA_TOPK = 3
RMS_EPS = 1e-6
NEG = -1e30
LOG2E = 1.4426950408889634

LANES = 128
FF_CHUNK = 512
FF_HEAD = D_FF % FF_CHUNK
N_FF_CHUNKS = D_FF // FF_CHUNK
ATT_T = 256
HEAD_AUG = 16
SUM_ROWS = 16
FFN_TM = 1024
PROJ_TM = 512
VMEM_LIMIT = 56 * 1024 * 1024

_BF16 = jnp.bfloat16
_F32 = jnp.float32


def _resident(shape):
    nd = len(shape)
    return pl.BlockSpec(shape, lambda *_: (0,) * nd, pipeline_mode=pl.Buffered(1))


def _layer(shape, l):
    nd = len(shape) - 1
    return pl.BlockSpec((None,) + tuple(shape[1:]), lambda *_: (l,) + (0,) * nd,
                        pipeline_mode=pl.Buffered(1))


def _rms(x, g):
    return x * lax.rsqrt(jnp.mean(x * x, axis=-1, keepdims=True) + RMS_EPS) * g


def _split3(x):
    hi = x.astype(_BF16)
    r1 = x - hi.astype(_F32)
    mid = r1.astype(_BF16)
    lo = (r1 - mid.astype(_F32)).astype(_BF16)
    return hi, mid, lo


def _ffn_kernel(*refs, with_out_proj):
    if with_out_proj:
        x_ref, fo_ref, mo_ref, wout_ref, g_ref, wgu_ref, wd_ref, o_ref, h_ref = refs
        mixed = jnp.concatenate([fo_ref[...], mo_ref[...]], axis=1)
        x = x_ref[...] + jnp.dot(mixed, wout_ref[...], preferred_element_type=_F32)
        o_ref[...] = x
        res_ref = o_ref
    else:
        x_ref, g_ref, wgu_ref, wd_ref, o_ref, h_ref = refs
        x = x_ref[...]
        res_ref = x_ref
    h_ref[...] = _rms(x, g_ref[...]).astype(_BF16)

    def chunk(lo, width):
        h = h_ref[...]
        gate = jnp.dot(h, wgu_ref[:, pl.ds(lo, width)], preferred_element_type=_F32)
        up = jnp.dot(h, wgu_ref[:, pl.ds(D_FF + lo, width)], preferred_element_type=_F32)
        act = (gate * jax.nn.sigmoid(gate) * up).astype(_BF16)
        return jnp.dot(act, wd_ref[pl.ds(lo, width), :], preferred_element_type=_F32)

    acc = chunk(0, FF_HEAD)
    for c in range(N_FF_CHUNKS):
        acc = acc + chunk(FF_HEAD + c * FF_CHUNK, FF_CHUNK)
    o_ref[...] = res_ref[...] + 0.5 * acc


def _ffn(x2, l, g, wgu, wd, out_proj=None):
    n = x2.shape[0]
    tm = FFN_TM
    tok = lambda w: pl.BlockSpec((tm, w), lambda i: (i, 0))
    in_specs = [tok(D_MODEL)]
    args = [x2]
    if out_proj is not None:
        fo, mo, wout = out_proj
        in_specs += [tok(GROUP_W), tok(GROUP_W), _layer(wout.shape, l)]
        args += [fo, mo, wout]
    in_specs += [_layer(g.shape, l), _layer(wgu.shape, l), _layer(wd.shape, l)]
    args += [g, wgu, wd]
    return pl.pallas_call(
        functools.partial(_ffn_kernel, with_out_proj=out_proj is not None),
        grid=(n // tm,),
        in_specs=in_specs,
        out_specs=tok(D_MODEL),
        out_shape=jax.ShapeDtypeStruct((n, D_MODEL), _F32),
        scratch_shapes=[pltpu.VMEM((tm, D_MODEL), _BF16)],
        compiler_params=pltpu.CompilerParams(
            dimension_semantics=("arbitrary",), vmem_limit_bytes=VMEM_LIMIT),
        name="ffn_out" if out_proj is not None else "ffn",
    )(*args)


def _proj_kernel(x_ref, g_ref, w_ref, gfq_ref, gfk_ref, gmq_ref, gmk_ref, bf_ref, pool_ref,
                 tri_ref, e_ref, onesq_ref, onesk_ref,
                 fq_ref, fk_ref, fvt_ref, mq_ref, mk_ref, mvt_ref, faq_ref, fak_ref,
                 carry_ref):
    @pl.when(pl.program_id(1) == 0)
    def _():
        carry_ref[...] = jnp.zeros_like(carry_ref)

    h = _rms(x_ref[0], g_ref[...]).astype(_BF16)
    proj = jnp.dot(h, w_ref[...], preferred_element_type=_F32)

    def head_norm(t, gain_ref):
        ms = jnp.dot((t * t).astype(_BF16), pool_ref[...], preferred_element_type=_F32)
        return (t * lax.rsqrt(ms + RMS_EPS) * gain_ref[...]).astype(_BF16)

    w = GROUP_W
    fq_ref[0] = head_norm(proj[:, 0 * w:1 * w], gfq_ref)
    fk_ref[0] = head_norm(proj[:, 1 * w:2 * w], gfk_ref)
    fvt_ref[0] = proj[:, 2 * w:3 * w].T.astype(_BF16)
    mq_ref[0] = head_norm(proj[:, 3 * w:4 * w], gmq_ref)
    mk_ref[0] = head_norm(proj[:, 4 * w:5 * w], gmk_ref)
    mvt_ref[0] = proj[:, 5 * w:6 * w].T.astype(_BF16)

    z = proj[:, 6 * w:] + bf_ref[...]
    logf = jnp.minimum(z, 0.0) - jnp.log(1.0 + jnp.exp(-jnp.abs(z)))
    lane = lax.broadcasted_iota(jnp.int32, logf.shape, 1)
    logf = jnp.where(lane < N_FOX, logf, 0.0)
    pieces = jnp.concatenate(_split3(logf), axis=1)
    part = jnp.dot(tri_ref[...], pieces, preferred_element_type=_F32)
    cum = part[:, :LANES] + part[:, LANES:2 * LANES] + part[:, 2 * LANES:] + carry_ref[...]
    carry_ref[...] = cum[-1:, :]
    cpieces = jnp.concatenate(_split3(cum * LOG2E), axis=1)
    aug = jnp.dot(cpieces, e_ref[...], preferred_element_type=_F32)
    faq_ref[0] = (aug[:, :LANES] + onesq_ref[...]).astype(_BF16)
    fak_ref[0] = (aug[:, LANES:] + onesk_ref[...]).astype(_BF16)


def _proj(x3, l, g, w_all, gains, bf, consts):
    b, s, _ = x3.shape
    tm = PROJ_TM
    row = pl.BlockSpec((1, tm, GROUP_W), lambda bi, i: (bi, i, 0))
    col = pl.BlockSpec((1, GROUP_W, tm), lambda bi, i: (bi, 0, i))
    aug = pl.BlockSpec((1, tm, LANES), lambda bi, i: (bi, i, 0))
    stacked = [g, w_all, *gains, bf]
    small = stacked + list(consts)
    return pl.pallas_call(
        _proj_kernel,
        grid=(b, s // tm),
        in_specs=[pl.BlockSpec((1, tm, D_MODEL), lambda bi, i: (bi, i, 0))]
        + [_layer(a.shape, l) for a in stacked] + [_resident(a.shape) for a in consts],
        out_specs=[row, row, col, row, row, col, aug, aug],
        out_shape=[jax.ShapeDtypeStruct((b, s, GROUP_W), _BF16),
                   jax.ShapeDtypeStruct((b, s, GROUP_W), _BF16),
                   jax.ShapeDtypeStruct((b, GROUP_W, s), _BF16),
                   jax.ShapeDtypeStruct((b, s, GROUP_W), _BF16),
                   jax.ShapeDtypeStruct((b, s, GROUP_W), _BF16),
                   jax.ShapeDtypeStruct((b, GROUP_W, s), _BF16),
                   jax.ShapeDtypeStruct((b, s, LANES), _BF16),
                   jax.ShapeDtypeStruct((b, s, LANES), _BF16)],
        scratch_shapes=[pltpu.VMEM((1, LANES), _F32)],
        compiler_params=pltpu.CompilerParams(
            dimension_semantics=("arbitrary", "arbitrary"), vmem_limit_bytes=VMEM_LIMIT),
        name="proj",
    )(x3, *small)


def _dot_nt(a, b):
    return lax.dot_general(a, b, (((1,), (1,)), ((), ())), preferred_element_type=_F32)


class _HeadGroup:
    def __init__(self, q_ref, k_ref, vt_ref, aq_ref, ak_ref, o_ref, qcat_ref, s_ref, acc_ref,
                 kmean_ref=None, selb_ref=None):
        self.moba = kmean_ref is not None
        self.batched_aug = len(aq_ref.shape) == 3
        (self.q_ref, self.k_ref, self.vt_ref, self.aq_ref, self.ak_ref, self.o_ref, self.qcat_ref,
         self.s_ref, self.acc_ref, self.kmean_ref, self.selb_ref) = (
            q_ref, k_ref, vt_ref, aq_ref, ak_ref, o_ref, qcat_ref, s_ref, acc_ref, kmean_ref,
            selb_ref)
        self.n_super = q_ref.shape[1] // (2 * ATT_T)
        self.stages = [(sb, j) for sb in range(self.n_super) for j in range(sb + 1)]
        self.chains = [(half, hh) for half in range(2) for hh in range(2)]
        self.m_run = [None] * 4
        self.ones_rows = jnp.ones((SUM_ROWS, 2 * ATT_T), _BF16)

    def k_rows(self, start, n):
        ak = (self.ak_ref[0, start:start + n, :] if self.batched_aug
              else self.ak_ref[start:start + n, :])
        return jnp.concatenate([self.k_ref[0, start:start + n, :], ak], axis=1)

    def vt_cols(self, hh, start, n):
        vt = self.vt_ref[0, hh * HEAD_DIM:(hh + 1) * HEAD_DIM, start:start + n]
        return jnp.concatenate([vt, self.ones_rows[:, :n]], axis=0)

    def sel_row(self, hh, q_tile, n):
        return self.selb_ref[hh, n:n + 1, q_tile * ATT_T:(q_tile + 1) * ATT_T]

    def prologue(self):
        t = ATT_T
        lane = lax.broadcasted_iota(jnp.int32, (1, LANES), 1)
        q = self.q_ref[0]
        aq = self.aq_ref[0] if self.batched_aug else self.aq_ref[...]
        if self.moba:
            nb = self.kmean_ref.shape[0]
            for n in range(nb):
                blk = self.k_ref[0, n * MOBA_BLOCK:(n + 1) * MOBA_BLOCK, :].astype(_F32)
                self.kmean_ref[n:n + 1, :] = jnp.mean(blk, axis=0, keepdims=True)
            km = jnp.concatenate(_split3(self.kmean_ref[...]), axis=0)
        for hh in range(2):
            in_head = (lane >= hh * HEAD_DIM) & (lane < (hh + 1) * HEAD_DIM)
            aug_lo = (2 * pl.program_id(1) + hh) * HEAD_AUG
            in_aug = (lane >= aug_lo) & (lane < aug_lo + HEAD_AUG)
            qh = jnp.where(in_head, q, jnp.zeros_like(q))
            qa = jnp.where(in_aug, aq, jnp.zeros_like(aq))
            for sb in range(self.n_super):
                for half in range(2):
                    r0 = (2 * sb + half) * t
                    self.qcat_ref[sb, 2 * half + hh] = jnp.concatenate(
                        [qh[r0:r0 + t], qa[r0:r0 + t]], axis=1)
            if self.moba:
                g3 = _dot_nt(km, qh)
                gate = g3[:nb] + g3[nb:2 * nb] + g3[2 * nb:]
                blk_id = lax.broadcasted_iota(jnp.int32, gate.shape, 0)
                n_past = lax.broadcasted_iota(jnp.int32, gate.shape, 1) // MOBA_BLOCK
                beaten = jnp.zeros(gate.shape, jnp.int32)
                for m in range(nb):
                    gm = gate[m:m + 1, :]
                    wins = (gm > gate) | ((gm == gate) & (blk_id > m))
                    beaten = beaten + jnp.where(wins & (n_past > m), 1, 0)
                keep = (beaten < MOBA_TOPK) & (blk_id < n_past)
                self.selb_ref[hh] = jnp.where(keep, 0.0, NEG)

    def issue_scores(self, n, c):
        if n >= len(self.stages):
            return
        t = ATT_T
        sb, j = self.stages[n]
        half = self.chains[c][0]
        nk = t if (j == sb and half == 0) else 2 * t
        kk = self.k_rows(2 * j * t, nk)
        self.s_ref[n % 2, c, :nk, :] = _dot_nt(kk, self.qcat_ref[sb, c])

    def process(self, n, c):
        t = ATT_T
        sb, j = self.stages[n]
        half, hh = self.chains[c]
        diag = j == sb
        q_tile = 2 * sb + half
        if diag:
            nk = (half + 1) * t
            s = self.s_ref[n % 2, c, :nk, :]
            key_pos = lax.broadcasted_iota(jnp.int32, s.shape, 0)
            qry_pos = lax.broadcasted_iota(jnp.int32, s.shape, 1) + half * t
            s = jnp.where(key_pos <= qry_pos, s, NEG)
            if self.moba and half == 1:
                s = jnp.concatenate([s[:t] + self.sel_row(hh, q_tile, 2 * j), s[t:]], axis=0)
        else:
            nk = 2 * t
            s = self.s_ref[n % 2, c]
            if self.moba:
                s = jnp.concatenate([s[:t] + self.sel_row(hh, q_tile, 2 * j),
                                     s[t:] + self.sel_row(hh, q_tile, 2 * j + 1)], axis=0)
        vv = self.vt_cols(hh, 2 * j * t, nk)
        m_blk = jnp.max(s, axis=0, keepdims=True)
        if j == 0:
            m_new = m_blk
            p = jnp.exp2(s - m_new).astype(_BF16)
            self.acc_ref[c] = jnp.dot(vv, p, preferred_element_type=_F32)
        else:
            m_new = jnp.maximum(self.m_run[c], m_blk)
            alpha = jnp.exp2(self.m_run[c] - m_new)
            p = jnp.exp2(s - m_new).astype(_BF16)
            self.acc_ref[c] = alpha * self.acc_ref[c] + jnp.dot(vv, p,
                                                                preferred_element_type=_F32)
        self.m_run[c] = m_new

    def finish_stage(self, n):
        t = ATT_T
        sb, j = self.stages[n]
        if j != sb:
            return
        for half in range(2):
            a0 = self.acc_ref[2 * half]
            a1 = self.acc_ref[2 * half + 1]
            out_t = jnp.concatenate([a0[:HEAD_DIM] / a0[HEAD_DIM:HEAD_DIM + 1],
                                     a1[:HEAD_DIM] / a1[HEAD_DIM:HEAD_DIM + 1]],
                                    axis=0)
            r0 = (2 * sb + half) * t
            self.o_ref[0, r0:r0 + t, :] = out_t.T.astype(_BF16)


def _attn_kernel(fq_ref, fk_ref, fvt_ref, faq_ref, fak_ref, mq_ref, mk_ref, mvt_ref, maq_ref,
                 mak_ref, fo_ref, mo_ref, fqcat_ref, fs_ref, facc_ref, mqcat_ref, ms_ref,
                 macc_ref, kmean_ref, selb_ref):
    groups = [_HeadGroup(fq_ref, fk_ref, fvt_ref, faq_ref, fak_ref, fo_ref, fqcat_ref, fs_ref,
                         facc_ref),
              _HeadGroup(mq_ref, mk_ref, mvt_ref, maq_ref, mak_ref, mo_ref, mqcat_ref, ms_ref,
                         macc_ref, kmean_ref, selb_ref)]
    for g in groups:
        g.prologue()
    for c in range(4):
        for g in groups:
            g.issue_scores(0, c)
    for n in range(len(groups[0].stages)):
        for c in range(4):
            for g in groups:
                g.issue_scores(n + 1, c)
                g.process(n, c)
        for g in groups:
            g.finish_stage(n)


def _attention(fq, fk, fvt, faq, fak, mq, mk, mvt, maq, mak):
    b, s, _ = fq.shape
    t = ATT_T
    npairs = GROUP_W // LANES
    seq_blk = pl.BlockSpec((1, s, LANES), lambda bi, p: (bi, 0, p))
    vt_blk = pl.BlockSpec((1, LANES, s), lambda bi, p: (bi, p, 0))
    faug_blk = pl.BlockSpec((1, s, LANES), lambda bi, p: (bi, 0, 0))
    maug_blk = pl.BlockSpec((s, LANES), lambda bi, p: (0, 0))
    group_scratch = [pltpu.VMEM((s // (2 * t), 4, t, 2 * LANES), _BF16),
                     pltpu.VMEM((2, 4, 2 * t, t), _F32),
                     pltpu.VMEM((4, HEAD_DIM + SUM_ROWS, t), _F32)]
    out = jax.ShapeDtypeStruct((b, s, GROUP_W), _BF16)
    return pl.pallas_call(
        _attn_kernel,
        grid=(b, npairs),
        in_specs=[seq_blk, seq_blk, vt_blk, faug_blk, faug_blk,
                  seq_blk, seq_blk, vt_blk, maug_blk, maug_blk],
        out_specs=[seq_blk, seq_blk],
        out_shape=[out, out],
        scratch_shapes=group_scratch + group_scratch
        + [pltpu.VMEM((s // MOBA_BLOCK, LANES), _F32),
           pltpu.VMEM((2, s // MOBA_BLOCK, s), _F32)],
        compiler_params=pltpu.CompilerParams(
            dimension_semantics=("arbitrary", "arbitrary"), vmem_limit_bytes=VMEM_LIMIT),
        name="attn",
    )(fq, fk, fvt, faq, fak, mq, mk, mvt, maq, mak)


def _aug_base(h):
    return h * HEAD_AUG


def _np_split3(x):
    x = np.asarray(x, np.float32)
    hi = x.astype(ml_dtypes.bfloat16)
    r1 = x - hi.astype(np.float32)
    mid = r1.astype(ml_dtypes.bfloat16)
    lo = (r1 - mid.astype(np.float32)).astype(ml_dtypes.bfloat16)
    return hi, mid, lo


def _constants(seq):
    idx = np.arange(GROUP_W)
    pool = (idx[:, None] // HEAD_DIM == idx[None, :] // HEAD_DIM).astype(np.float32) / HEAD_DIM
    r = np.arange(PROJ_TM)
    tri = (r[None, :] <= r[:, None]).astype(np.float32)
    e = np.zeros((3 * LANES, 2 * LANES), np.float32)
    onesq = np.zeros((1, LANES), np.float32)
    onesk = np.zeros((1, LANES), np.float32)
    for h in range(N_FOX):
        base = _aug_base(h)
        for piece in range(3):
            e[piece * LANES + h, base + piece] = 1.0
            e[piece * LANES + h, LANES + base + 3 + piece] = -1.0
            onesq[0, base + 3 + piece] = 1.0
            onesk[0, base + piece] = 1.0
    pos = np.arange(seq, dtype=np.float32)
    taq = np.zeros((seq, LANES), ml_dtypes.bfloat16)
    tak = np.zeros((seq, LANES), ml_dtypes.bfloat16)
    for h in range(N_MOBA):
        base = _aug_base(h)
        slope = np.float32(2.0) ** np.float32(-8.0 * (h + 1) / N_MOBA)
        ramp = slope * pos * np.float32(LOG2E)
        for piece, (qv, kv) in enumerate(zip(_np_split3(-ramp), _np_split3(ramp))):
            taq[:, base + piece] = qv
            tak[:, base + 3 + piece] = kv
            taq[:, base + 3 + piece] = 1.0
            tak[:, base + piece] = 1.0
    bf = lambda a: jnp.asarray(a, _BF16)
    return ((bf(pool), bf(tri), bf(e), jnp.asarray(onesq), jnp.asarray(onesk)),
            (jnp.asarray(taq), jnp.asarray(tak)))


def _prep_w_in(w_in):
    w = GROUP_W
    o = 3 * w + N_FOX
    pad = jnp.zeros(w_in.shape[:2] + (LANES - N_FOX,), w_in.dtype)
    cols = [w_in[..., :3 * w], w_in[..., o:o + 3 * w], w_in[..., 3 * w:o], pad]
    return jnp.concatenate(cols, axis=-1).astype(_BF16)


def kernel(x, ffn1_norm_g, ffn1_w_gu, ffn1_w_down, mix_norm_g, w_in, b_f, fox_q_norm_g,
           fox_k_norm_g, moba_q_norm_g, moba_k_norm_g, w_out, ffn2_norm_g, ffn2_w_gu,
           ffn2_w_down):
    b, s, d = x.shape
    depth = w_in.shape[0]
    assert d == D_MODEL and s % (2 * ATT_T) == 0 and (b * s) % FFN_TM == 0
    proj_consts, (taq, tak) = _constants(s)
    q_scale = np.float32(HEAD_DIM ** -0.5 * LOG2E)
    tile = lambda g: jnp.tile(g, (1, GROUP_W // HEAD_DIM))[:, None, :]
    row = lambda g: g[:, None, :]

    ffn1 = (row(ffn1_norm_g), ffn1_w_gu.astype(_BF16), ffn1_w_down.astype(_BF16))
    ffn2 = (row(ffn2_norm_g), ffn2_w_gu.astype(_BF16), ffn2_w_down.astype(_BF16))
    w_out_b = w_out.astype(_BF16)
    w_all = _prep_w_in(w_in)
    gains = [tile(fox_q_norm_g) * q_scale, tile(fox_k_norm_g),
             tile(moba_q_norm_g) * q_scale, tile(moba_k_norm_g)]
    bf = jnp.pad(b_f, ((0, 0), (0, LANES - N_FOX)))[:, None, :]
    mix_g = row(mix_norm_g)

    x2 = x.reshape(b * s, d)
    for l in range(depth):
        x2 = _ffn(x2, l, *ffn1)
        fq, fk, fvt, mq, mk, mvt, faq, fak = _proj(
            x2.reshape(b, s, d), l, mix_g, w_all, gains, bf, proj_consts)
        fo, mo = _attention(fq, fk, fvt, faq, fak, mq, mk, mvt, taq, tak)
        x2 = _ffn(x2, l, *ffn2,
                  out_proj=(fo.reshape(b * s, GROUP_W), mo.reshape(b * s, GROUP_W), w_out_b))
    return x2.reshape(b, s, d)
```

```python
import functools

import numpy as np
import ml_dtypes
import jax
import jax.numpy as jnp
from jax import lax
from jax.experimental import pallas as pl
from jax.experimental.pallas import tpu as pltpu

D_MODEL = 1024
HEAD_DIM = 64
N_FOX = 8
N_MOBA = 8
GROUP_W = 512
D_FF = 2816
MOBA_BLOCK = 256
MOBA_TOPK = 3
RMS_EPS = 1e-6
NEG = -1e30
LOG2E = 1.4426950408889634

LANES = 128
FF_CHUNK = 512
FF_HEAD = D_FF % FF_CHUNK
N_FF_CHUNKS = D_FF // FF_CHUNK
ATT_T = 256
HEAD_AUG = 16
SUM_ROWS = 16
FFN_TM = 1024
PROJ_TM = 512
VMEM_LIMIT = 56 * 1024 * 1024

_BF16 = jnp.bfloat16
_F32 = jnp.float32


def _resident(shape):
    nd = len(shape)
    return pl.BlockSpec(shape, lambda *_: (0,) * nd, pipeline_mode=pl.Buffered(1))


def _layer(shape, l):
    nd = len(shape) - 1
    return pl.BlockSpec((None,) + tuple(shape[1:]), lambda *_: (l,) + (0,) * nd,
                        pipeline_mode=pl.Buffered(1))


def _rms(x, g):
    return x * lax.rsqrt(jnp.mean(x * x, axis=-1, keepdims=True) + RMS_EPS) * g


def _split3(x):
    hi = x.astype(_BF16)
    r1 = x - hi.astype(_F32)
    mid = r1.astype(_BF16)
    lo = (r1 - mid.astype(_F32)).astype(_BF16)
    return hi, mid, lo


def _ffn_kernel(*refs, with_out_proj):
    if with_out_proj:
        x_ref, fo_ref, mo_ref, wout_ref, g_ref, wgu_ref, wd_ref, o_ref, h_ref = refs
        mixed = jnp.concatenate([fo_ref[...], mo_ref[...]], axis=1)
        x = x_ref[...] + jnp.dot(mixed, wout_ref[...], preferred_element_type=_F32)
        o_ref[...] = x
        res_ref = o_ref
    else:
        x_ref, g_ref, wgu_ref, wd_ref, o_ref, h_ref = refs
        x = x_ref[...]
        res_ref = x_ref
    h_ref[...] = _rms(x, g_ref[...]).astype(_BF16)

    def chunk(lo, width):
        h = h_ref[...]
        gate = jnp.dot(h, wgu_ref[:, pl.ds(lo, width)], preferred_element_type=_F32)
        up = jnp.dot(h, wgu_ref[:, pl.ds(D_FF + lo, width)], preferred_element_type=_F32)
        act = (gate * jax.nn.sigmoid(gate) * up).astype(_BF16)
        return jnp.dot(act, wd_ref[pl.ds(lo, width), :], preferred_element_type=_F32)

    acc = chunk(0, FF_HEAD)
    for c in range(N_FF_CHUNKS):
        acc = acc + chunk(FF_HEAD + c * FF_CHUNK, FF_CHUNK)
    o_ref[...] = res_ref[...] + 0.5 * acc


def _ffn(x2, l, g, wgu, wd, out_proj=None):
    n = x2.shape[0]
    tm = FFN_TM
    tok = lambda w: pl.BlockSpec((tm, w), lambda i: (i, 0))
    in_specs = [tok(D_MODEL)]
    args = [x2]
    if out_proj is not None:
        fo, mo, wout = out_proj
        in_specs += [tok(GROUP_W), tok(GROUP_W), _layer(wout.shape, l)]
        args += [fo, mo, wout]
    in_specs += [_layer(g.shape, l), _layer(wgu.shape, l), _layer(wd.shape, l)]
    args += [g, wgu, wd]
    return pl.pallas_call(
        functools.partial(_ffn_kernel, with_out_proj=out_proj is not None),
        grid=(n // tm,),
        in_specs=in_specs,
        out_specs=tok(D_MODEL),
        out_shape=jax.ShapeDtypeStruct((n, D_MODEL), _F32),
        scratch_shapes=[pltpu.VMEM((tm, D_MODEL), _BF16)],
        compiler_params=pltpu.CompilerParams(
            dimension_semantics=("arbitrary",), vmem_limit_bytes=VMEM_LIMIT),
        name="ffn_out" if out_proj is not None else "ffn",
    )(*args)


def _proj_kernel(x_ref, g_ref, w_ref, gfq_ref, gfk_ref, gmq_ref, gmk_ref, bf_ref, pool_ref,
                 tri_ref, e_ref, onesq_ref, onesk_ref,
                 fqt_ref, fk_ref, fvt_ref, mqt_ref, mk_ref, mvt_ref, faqt_ref, fak_ref,
                 carry_ref):
    @pl.when(pl.program_id(1) == 0)
    def _():
        carry_ref[...] = jnp.zeros_like(carry_ref)

    h = _rms(x_ref[0], g_ref[...]).astype(_BF16)
    proj = jnp.dot(h, w_ref[...], preferred_element_type=_F32)

    def head_norm(t, gain_ref):
        ms = jnp.dot((t * t).astype(_BF16), pool_ref[...], preferred_element_type=_F32)
        return (t * lax.rsqrt(ms + RMS_EPS) * gain_ref[...]).astype(_BF16)

    def head_norm_t(t, gain_ref):
        tt = t.T
        heads = []
        for hd in range(GROUP_W // HEAD_DIM):
            blk = tt[hd * HEAD_DIM:(hd + 1) * HEAD_DIM]
            ms = jnp.mean(blk * blk, axis=0, keepdims=True)
            heads.append(blk * lax.rsqrt(ms + RMS_EPS) * gain_ref[...])
        return jnp.concatenate(heads, axis=0).astype(_BF16)

    w = GROUP_W
    fqt_ref[0] = head_norm_t(proj[:, 0 * w:1 * w], gfq_ref)
    fk_ref[0] = head_norm(proj[:, 1 * w:2 * w], gfk_ref)
    fvt_ref[0] = proj[:, 2 * w:3 * w].T.astype(_BF16)
    mqt_ref[0] = head_norm_t(proj[:, 3 * w:4 * w], gmq_ref)
    mk_ref[0] = head_norm(proj[:, 4 * w:5 * w], gmk_ref)
    mvt_ref[0] = proj[:, 5 * w:6 * w].T.astype(_BF16)

    z = proj[:, 6 * w:] + bf_ref[...]
    logf = jnp.minimum(z, 0.0) - jnp.log(1.0 + jnp.exp(-jnp.abs(z)))
    lane = lax.broadcasted_iota(jnp.int32, logf.shape, 1)
    logf = jnp.where(lane < N_FOX, logf, 0.0)
    pieces = jnp.concatenate(_split3(logf), axis=1)
    part = jnp.dot(tri_ref[...], pieces, preferred_element_type=_F32)
    cum = part[:, :LANES] + part[:, LANES:2 * LANES] + part[:, 2 * LANES:] + carry_ref[...]
    carry_ref[...] = cum[-1:, :]
    cpieces = jnp.concatenate(_split3(cum * LOG2E), axis=1)
    aug = jnp.dot(cpieces, e_ref[...], preferred_element_type=_F32)
    faqt_ref[0] = (aug[:, :LANES] + onesq_ref[...]).T.astype(_BF16)
    fak_ref[0] = (aug[:, LANES:] + onesk_ref[...]).astype(_BF16)


def _proj(x3, l, g, w_all, gains, bf, consts):
    b, s, _ = x3.shape
    tm = PROJ_TM
    row = pl.BlockSpec((1, tm, GROUP_W), lambda bi, i: (bi, i, 0))
    col = pl.BlockSpec((1, GROUP_W, tm), lambda bi, i: (bi, 0, i))
    aug = pl.BlockSpec((1, tm, LANES), lambda bi, i: (bi, i, 0))
    aug_t = pl.BlockSpec((1, LANES, tm), lambda bi, i: (bi, 0, i))
    stacked = [g, w_all, *gains, bf]
    small = stacked + list(consts)
    return pl.pallas_call(
        _proj_kernel,
        grid=(b, s // tm),
        in_specs=[pl.BlockSpec((1, tm, D_MODEL), lambda bi, i: (bi, i, 0))]
        + [_layer(a.shape, l) for a in stacked] + [_resident(a.shape) for a in consts],
        out_specs=[col, row, col, col, row, col, aug_t, aug],
        out_shape=[jax.ShapeDtypeStruct((b, GROUP_W, s), _BF16),
                   jax.ShapeDtypeStruct((b, s, GROUP_W), _BF16),
                   jax.ShapeDtypeStruct((b, GROUP_W, s), _BF16),
                   jax.ShapeDtypeStruct((b, GROUP_W, s), _BF16),
                   jax.ShapeDtypeStruct((b, s, GROUP_W), _BF16),
                   jax.ShapeDtypeStruct((b, GROUP_W, s), _BF16),
                   jax.ShapeDtypeStruct((b, LANES, s), _BF16),
                   jax.ShapeDtypeStruct((b, s, LANES), _BF16)],
        scratch_shapes=[pltpu.VMEM((1, LANES), _F32)],
        compiler_params=pltpu.CompilerParams(
            dimension_semantics=("arbitrary", "arbitrary"), vmem_limit_bytes=VMEM_LIMIT),
        name="proj",
    )(x3, *small)


class _HeadGroup:
    def __init__(self, qt_ref, k_ref, vt_ref, aqt_ref, ak_ref, o_ref, qcat_ref, s_ref, acc_ref,
                 kmean_ref=None, selb_ref=None):
        self.moba = kmean_ref is not None
        self.batched_aug = len(ak_ref.shape) == 3
        (self.qt_ref, self.k_ref, self.vt_ref, self.aqt_ref, self.ak_ref, self.o_ref,
         self.qcat_ref, self.s_ref, self.acc_ref, self.kmean_ref, self.selb_ref) = (
            qt_ref, k_ref, vt_ref, aqt_ref, ak_ref, o_ref, qcat_ref, s_ref, acc_ref, kmean_ref,
            selb_ref)
        self.n_super = k_ref.shape[1] // (2 * ATT_T)
        self.stages = [(sb, j) for sb in range(self.n_super) for j in range(sb + 1)]
        self.chains = [(half, hh) for half in range(2) for hh in range(2)]
        self.m_run = [None] * 4
        self.ones_rows = jnp.ones((SUM_ROWS, 2 * ATT_T), _BF16)

    def k_rows(self, start, n):
        ak = (self.ak_ref[0, start:start + n, :] if self.batched_aug
              else self.ak_ref[start:start + n, :])
        return jnp.concatenate([self.k_ref[0, start:start + n, :], ak], axis=1)

    def vt_cols(self, hh, start, n):
        vt = self.vt_ref[0, hh * HEAD_DIM:(hh + 1) * HEAD_DIM, start:start + n]
        return jnp.concatenate([vt, self.ones_rows[:, :n]], axis=0)

    def sel_row(self, hh, q_tile, n):
        return self.selb_ref[hh, n:n + 1, q_tile * ATT_T:(q_tile + 1) * ATT_T]

    def prologue(self):
        t = ATT_T
        row = lax.broadcasted_iota(jnp.int32, (LANES, 1), 0)
        qt = self.qt_ref[0]
        aqt = self.aqt_ref[0] if self.batched_aug else self.aqt_ref[...]
        zeros = jnp.zeros((HEAD_DIM, qt.shape[1]), _BF16)
        if self.moba:
            nb = self.kmean_ref.shape[0]
            for n in range(nb):
                blk = self.k_ref[0, n * MOBA_BLOCK:(n + 1) * MOBA_BLOCK, :].astype(_F32)
                self.kmean_ref[n:n + 1, :] = jnp.mean(blk, axis=0, keepdims=True)
            km = jnp.concatenate(_split3(self.kmean_ref[...]), axis=0)
        for hh in range(2):
            aug_lo = (2 * pl.program_id(1) + hh) * HEAD_AUG
            in_aug = (row >= aug_lo) & (row < aug_lo + HEAD_AUG)
            qa = jnp.where(in_aug, aqt, jnp.zeros_like(aqt))
            own = qt[hh * HEAD_DIM:(hh + 1) * HEAD_DIM]
            qh = jnp.concatenate([own, zeros] if hh == 0 else [zeros, own], axis=0)
            for sb in range(self.n_super):
                for half in range(2):
                    r0 = (2 * sb + half) * t
                    self.qcat_ref[sb, 2 * half + hh] = jnp.concatenate(
                        [qh[:, r0:r0 + t], qa[:, r0:r0 + t]], axis=0)
            if self.moba:
                g3 = jnp.dot(km, qh, preferred_element_type=_F32)
                gate = g3[:nb] + g3[nb:2 * nb] + g3[2 * nb:]
                blk_id = lax.broadcasted_iota(jnp.int32, gate.shape, 0)
                n_past = lax.broadcasted_iota(jnp.int32, gate.shape, 1) // MOBA_BLOCK
                beaten = jnp.zeros(gate.shape, jnp.int32)
                for m in range(nb):
                    gm = gate[m:m + 1, :]
                    wins = (gm > gate) | ((gm == gate) & (blk_id > m))
                    beaten = beaten + jnp.where(wins & (n_past > m), 1, 0)
                keep = (beaten < MOBA_TOPK) & (blk_id < n_past)
                self.selb_ref[hh] = jnp.where(keep, 0.0, NEG)

    def issue_scores(self, n, c):
        if n >= len(self.stages):
            return
        t = ATT_T
        sb, j = self.stages[n]
        half = self.chains[c][0]
        nk = t if (j == sb and half == 0) else 2 * t
        kk = self.k_rows(2 * j * t, nk)
        self.s_ref[n % 2, c, :nk, :] = jnp.dot(kk, self.qcat_ref[sb, c],
                                               preferred_element_type=_F32)

    def process(self, n, c):
        t = ATT_T
        sb, j = self.stages[n]
        half, hh = self.chains[c]
        diag = j == sb
        q_tile = 2 * sb + half
        if diag:
            nk = (half + 1) * t
            s = self.s_ref[n % 2, c, :nk, :]
            key_pos = lax.broadcasted_iota(jnp.int32, s.shape, 0)
            qry_pos = lax.broadcasted_iota(jnp.int32, s.shape, 1) + half * t
            s = jnp.where(key_pos <= qry_pos, s, NEG)
            if self.moba and half == 1:
                s = jnp.concatenate([s[:t] + self.sel_row(hh, q_tile, 2 * j), s[t:]], axis=0)
        else:
            nk = 2 * t
            s = self.s_ref[n % 2, c]
            if self.moba:
                s = jnp.concatenate([s[:t] + self.sel_row(hh, q_tile, 2 * j),
                                     s[t:] + self.sel_row(hh, q_tile, 2 * j + 1)], axis=0)
        vv = self.vt_cols(hh, 2 * j * t, nk)
        m_blk = jnp.max(s, axis=0, keepdims=True)
        if j == 0:
            m_new = m_blk
            p = jnp.exp2(s - m_new).astype(_BF16)
            self.acc_ref[c] = jnp.dot(vv, p, preferred_element_type=_F32)
        else:
            m_new = jnp.maximum(self.m_run[c], m_blk)
            alpha = jnp.exp2(self.m_run[c] - m_new)
            p = jnp.exp2(s - m_new).astype(_BF16)
            self.acc_ref[c] = alpha * self.acc_ref[c] + jnp.dot(vv, p,
                                                                preferred_element_type=_F32)
        self.m_run[c] = m_new

    def finish_stage(self, n):
        t = ATT_T
        sb, j = self.stages[n]
        if j != sb:
            return
        for half in range(2):
            a0 = self.acc_ref[2 * half]
            a1 = self.acc_ref[2 * half + 1]
            out_t = jnp.concatenate([a0[:HEAD_DIM] / a0[HEAD_DIM:HEAD_DIM + 1],
                                     a1[:HEAD_DIM] / a1[HEAD_DIM:HEAD_DIM + 1]],
                                    axis=0)
            r0 = (2 * sb + half) * t
            self.o_ref[0, r0:r0 + t, :] = out_t.T.astype(_BF16)


def _attn_kernel(fqt_ref, fk_ref, fvt_ref, faqt_ref, fak_ref, mqt_ref, mk_ref, mvt_ref, maqt_ref,
                 mak_ref, fo_ref, mo_ref, fqcat_ref, fs_ref, facc_ref, mqcat_ref, ms_ref,
                 macc_ref, kmean_ref, selb_ref):
    groups = [_HeadGroup(fqt_ref, fk_ref, fvt_ref, faqt_ref, fak_ref, fo_ref, fqcat_ref, fs_ref,
                         facc_ref),
              _HeadGroup(mqt_ref, mk_ref, mvt_ref, maqt_ref, mak_ref, mo_ref, mqcat_ref, ms_ref,
                         macc_ref, kmean_ref, selb_ref)]
    for g in groups:
        g.prologue()
    for c in range(4):
        for g in groups:
            g.issue_scores(0, c)
    for n in range(len(groups[0].stages)):
        for c in range(4):
            for g in groups:
                g.issue_scores(n + 1, c)
                g.process(n, c)
        for g in groups:
            g.finish_stage(n)


def _attention(fqt, fk, fvt, faqt, fak, mqt, mk, mvt, maqt, mak):
    b, s, _ = fk.shape
    t = ATT_T
    npairs = GROUP_W // LANES
    seq_blk = pl.BlockSpec((1, s, LANES), lambda bi, p: (bi, 0, p))
    vt_blk = pl.BlockSpec((1, LANES, s), lambda bi, p: (bi, p, 0))
    faug_blk = pl.BlockSpec((1, s, LANES), lambda bi, p: (bi, 0, 0))
    faugt_blk = pl.BlockSpec((1, LANES, s), lambda bi, p: (bi, 0, 0))
    maug_blk = pl.BlockSpec((s, LANES), lambda bi, p: (0, 0))
    maugt_blk = pl.BlockSpec((LANES, s), lambda bi, p: (0, 0))
    group_scratch = [pltpu.VMEM((s // (2 * t), 4, 2 * LANES, t), _BF16),
                     pltpu.VMEM((2, 4, 2 * t, t), _F32),
                     pltpu.VMEM((4, HEAD_DIM + SUM_ROWS, t), _F32)]
    out = jax.ShapeDtypeStruct((b, s, GROUP_W), _BF16)
    return pl.pallas_call(
        _attn_kernel,
        grid=(b, npairs),
        in_specs=[vt_blk, seq_blk, vt_blk, faugt_blk, faug_blk,
                  vt_blk, seq_blk, vt_blk, maugt_blk, maug_blk],
        out_specs=[seq_blk, seq_blk],
        out_shape=[out, out],
        scratch_shapes=group_scratch + group_scratch
        + [pltpu.VMEM((s // MOBA_BLOCK, LANES), _F32),
           pltpu.VMEM((2, s // MOBA_BLOCK, s), _F32)],
        compiler_params=pltpu.CompilerParams(
            dimension_semantics=("arbitrary", "arbitrary"), vmem_limit_bytes=VMEM_LIMIT),
        name="attn",
    )(fqt, fk, fvt, faqt, fak, mqt, mk, mvt, maqt, mak)


def _aug_base(h):
    return h * HEAD_AUG


def _np_split3(x):
    x = np.asarray(x, np.float32)
    hi = x.astype(ml_dtypes.bfloat16)
    r1 = x - hi.astype(np.float32)
    mid = r1.astype(ml_dtypes.bfloat16)
    lo = (r1 - mid.astype(np.float32)).astype(ml_dtypes.bfloat16)
    return hi, mid, lo


def _constants(seq):
    idx = np.arange(GROUP_W)
    pool = (idx[:, None] // HEAD_DIM == idx[None, :] // HEAD_DIM).astype(np.float32) / HEAD_DIM
    r = np.arange(PROJ_TM)
    tri = (r[None, :] <= r[:, None]).astype(np.float32)
    e = np.zeros((3 * LANES, 2 * LANES), np.float32)
    onesq = np.zeros((1, LANES), np.float32)
    onesk = np.zeros((1, LANES), np.float32)
    for h in range(N_FOX):
        base = _aug_base(h)
        for piece in range(3):
            e[piece * LANES + h, base + piece] = 1.0
            e[piece * LANES + h, LANES + base + 3 + piece] = -1.0
            onesq[0, base + 3 + piece] = 1.0
            onesk[0, base + piece] = 1.0
    pos = np.arange(seq, dtype=np.float32)
    taq = np.zeros((seq, LANES), ml_dtypes.bfloat16)
    tak = np.zeros((seq, LANES), ml_dtypes.bfloat16)
    for h in range(N_MOBA):
        base = _aug_base(h)
        slope = np.float32(2.0) ** np.float32(-8.0 * (h + 1) / N_MOBA)
        ramp = slope * pos * np.float32(LOG2E)
        for piece, (qv, kv) in enumerate(zip(_np_split3(-ramp), _np_split3(ramp))):
            taq[:, base + piece] = qv
            tak[:, base + 3 + piece] = kv
            taq[:, base + 3 + piece] = 1.0
            tak[:, base + piece] = 1.0
    bf = lambda a: jnp.asarray(a, _BF16)
    return ((bf(pool), bf(tri), bf(e), jnp.asarray(onesq), jnp.asarray(onesk)),
            (jnp.asarray(np.ascontiguousarray(taq.T)), jnp.asarray(tak)))


def _prep_w_in(w_in):
    w = GROUP_W
    o = 3 * w + N_FOX
    pad = jnp.zeros(w_in.shape[:2] + (LANES - N_FOX,), w_in.dtype)
    cols = [w_in[..., :3 * w], w_in[..., o:o + 3 * w], w_in[..., 3 * w:o], pad]
    return jnp.concatenate(cols, axis=-1).astype(_BF16)


def kernel(x, ffn1_norm_g, ffn1_w_gu, ffn1_w_down, mix_norm_g, w_in, b_f, fox_q_norm_g,
           fox_k_norm_g, moba_q_norm_g, moba_k_norm_g, w_out, ffn2_norm_g, ffn2_w_gu,
           ffn2_w_down):
    b, s, d = x.shape
    depth = w_in.shape[0]
    assert d == D_MODEL and s % (2 * ATT_T) == 0 and (b * s) % FFN_TM == 0
    proj_consts, (taqt, tak) = _constants(s)
    q_scale = np.float32(HEAD_DIM ** -0.5 * LOG2E)
    tile = lambda g: jnp.tile(g, (1, GROUP_W // HEAD_DIM))[:, None, :]
    row = lambda g: g[:, None, :]

    ffn1 = (row(ffn1_norm_g), ffn1_w_gu.astype(_BF16), ffn1_w_down.astype(_BF16))
    ffn2 = (row(ffn2_norm_g), ffn2_w_gu.astype(_BF16), ffn2_w_down.astype(_BF16))
    w_out_b = w_out.astype(_BF16)
    w_all = _prep_w_in(w_in)
    col = lambda g: g[:, :, None]
    gains = [col(fox_q_norm_g) * q_scale, tile(fox_k_norm_g),
             col(moba_q_norm_g) * q_scale, tile(moba_k_norm_g)]
    bf = jnp.pad(b_f, ((0, 0), (0, LANES - N_FOX)))[:, None, :]
    mix_g = row(mix_norm_g)

    x2 = x.reshape(b * s, d)
    for l in range(depth):
        x2 = _ffn(x2, l, *ffn1)
        fqt, fk, fvt, mqt, mk, mvt, faqt, fak = _proj(
            x2.reshape(b, s, d), l, mix_g, w_all, gains, bf, proj_consts)
        fo, mo = _attention(fqt, fk, fvt, faqt, fak, mqt, mk, mvt, taqt, tak)
        x2 = _ffn(x2, l, *ffn2,
                  out_proj=(fo.reshape(b * s, GROUP_W), mo.reshape(b * s, GROUP_W), w_out_b))
    return x2.reshape(b, s, d)
```

```python
import functools

import numpy as np
import ml_dtypes
import jax
import jax.numpy as jnp
from jax import lax
from jax.experimental import pallas as pl
from jax.experimental.pallas import tpu as pltpu

D_MODEL = 1024
HEAD_DIM = 64
N_FOX = 8
N_MOBA = 8
GROUP_W = 512
D_FF = 2816
MOBA_BLOCK = 256
MOBA_TOPK = 3
RMS_EPS = 1e-6
NEG = -1e30
LOG2E = 1.4426950408889634

LANES = 128
FF_CHUNK = 512
FF_HEAD = D_FF % FF_CHUNK
N_FF_CHUNKS = D_FF // FF_CHUNK
ATT_T = 256
HEAD_AUG = 16
SEL_LANE = 8
SUM_ROWS = 16
FFN_TM = 1024
PROJ_TM = 512
VMEM_LIMIT = 56 * 1024 * 1024

_BF16 = jnp.bfloat16
_F32 = jnp.float32


def _resident(shape):
    nd = len(shape)
    return pl.BlockSpec(shape, lambda *_: (0,) * nd, pipeline_mode=pl.Buffered(1))


def _layer(shape, l):
    nd = len(shape) - 1
    return pl.BlockSpec((None,) + tuple(shape[1:]), lambda *_: (l,) + (0,) * nd,
                        pipeline_mode=pl.Buffered(1))


def _rms(x, g):
    return x * lax.rsqrt(jnp.mean(x * x, axis=-1, keepdims=True) + RMS_EPS) * g


def _split3(x):
    hi = x.astype(_BF16)
    r1 = x - hi.astype(_F32)
    mid = r1.astype(_BF16)
    lo = (r1 - mid.astype(_F32)).astype(_BF16)
    return hi, mid, lo


def _ffn_kernel(*refs, with_out_proj):
    if with_out_proj:
        x_ref, fo_ref, mo_ref, wout_ref, g_ref, wgu_ref, wd_ref, o_ref, h_ref = refs
        mixed = jnp.concatenate([fo_ref[...], mo_ref[...]], axis=1)
        x = x_ref[...] + jnp.dot(mixed, wout_ref[...], preferred_element_type=_F32)
        o_ref[...] = x
        res_ref = o_ref
    else:
        x_ref, g_ref, wgu_ref, wd_ref, o_ref, h_ref = refs
        x = x_ref[...]
        res_ref = x_ref
    h_ref[...] = _rms(x, g_ref[...]).astype(_BF16)

    def chunk(lo, width):
        h = h_ref[...]
        gate = jnp.dot(h, wgu_ref[:, pl.ds(lo, width)], preferred_element_type=_F32)
        up = jnp.dot(h, wgu_ref[:, pl.ds(D_FF + lo, width)], preferred_element_type=_F32)
        act = (gate * jax.nn.sigmoid(gate) * up).astype(_BF16)
        return jnp.dot(act, wd_ref[pl.ds(lo, width), :], preferred_element_type=_F32)

    acc = chunk(0, FF_HEAD)
    for c in range(N_FF_CHUNKS):
        acc = acc + chunk(FF_HEAD + c * FF_CHUNK, FF_CHUNK)
    o_ref[...] = res_ref[...] + 0.5 * acc


def _ffn(x2, l, g, wgu, wd, out_proj=None):
    n = x2.shape[0]
    tm = FFN_TM
    tok = lambda w: pl.BlockSpec((tm, w), lambda i: (i, 0))
    in_specs = [tok(D_MODEL)]
    args = [x2]
    if out_proj is not None:
        fo, mo, wout = out_proj
        in_specs += [tok(GROUP_W), tok(GROUP_W), _layer(wout.shape, l)]
        args += [fo, mo, wout]
    in_specs += [_layer(g.shape, l), _layer(wgu.shape, l), _layer(wd.shape, l)]
    args += [g, wgu, wd]
    return pl.pallas_call(
        functools.partial(_ffn_kernel, with_out_proj=out_proj is not None),
        grid=(n // tm,),
        in_specs=in_specs,
        out_specs=tok(D_MODEL),
        out_shape=jax.ShapeDtypeStruct((n, D_MODEL), _F32),
        scratch_shapes=[pltpu.VMEM((tm, D_MODEL), _BF16)],
        compiler_params=pltpu.CompilerParams(
            dimension_semantics=("arbitrary",), vmem_limit_bytes=VMEM_LIMIT),
        name="ffn_out" if out_proj is not None else "ffn",
    )(*args)


def _proj_kernel(x_ref, g_ref, w_ref, gfq_ref, gfk_ref, gmq_ref, gmk_ref, bf_ref,
                 tri_ref, e_ref, onesq_ref, onesk_ref,
                 fqt_ref, fk_ref, fvt_ref, mqt_ref, mk_ref, mvt_ref, faqt_ref, fak_ref,
                 carry_ref):
    @pl.when(pl.program_id(1) == 0)
    def _():
        carry_ref[...] = jnp.zeros_like(carry_ref)

    h = _rms(x_ref[0], g_ref[...]).astype(_BF16)
    proj = jnp.dot(h, w_ref[...], preferred_element_type=_F32)

    def head_norm_t(t, gain_ref):
        tt = t.T
        heads = []
        for hd in range(GROUP_W // HEAD_DIM):
            blk = tt[hd * HEAD_DIM:(hd + 1) * HEAD_DIM]
            ms = jnp.mean(blk * blk, axis=0, keepdims=True)
            heads.append(blk * lax.rsqrt(ms + RMS_EPS) * gain_ref[...])
        return jnp.concatenate(heads, axis=0)

    w = GROUP_W
    fqt_ref[0] = head_norm_t(proj[:, 0 * w:1 * w], gfq_ref).astype(_BF16)
    fk_ref[0] = head_norm_t(proj[:, 1 * w:2 * w], gfk_ref).T.astype(_BF16)
    fvt_ref[0] = proj[:, 2 * w:3 * w].T.astype(_BF16)
    mqt_ref[0] = head_norm_t(proj[:, 3 * w:4 * w], gmq_ref).astype(_BF16)
    mk_ref[0] = head_norm_t(proj[:, 4 * w:5 * w], gmk_ref).T.astype(_BF16)
    mvt_ref[0] = proj[:, 5 * w:6 * w].T.astype(_BF16)

    z = proj[:, 6 * w:] + bf_ref[...]
    logf = jnp.minimum(z, 0.0) - jnp.log(1.0 + jnp.exp(-jnp.abs(z)))
    lane = lax.broadcasted_iota(jnp.int32, logf.shape, 1)
    logf = jnp.where(lane < N_FOX, logf, 0.0)
    pieces = jnp.concatenate(_split3(logf), axis=1)
    part = jnp.dot(tri_ref[...], pieces, preferred_element_type=_F32)
    cum = part[:, :LANES] + part[:, LANES:2 * LANES] + part[:, 2 * LANES:] + carry_ref[...]
    carry_ref[...] = cum[-1:, :]
    cpieces = jnp.concatenate(_split3(cum * LOG2E), axis=1)
    aug = jnp.dot(cpieces, e_ref[...], preferred_element_type=_F32)
    faqt_ref[0] = (aug[:, :LANES] + onesq_ref[...]).T.astype(_BF16)
    fak_ref[0] = (aug[:, LANES:] + onesk_ref[...]).astype(_BF16)


def _proj(x3, l, g, w_all, gains, bf, consts):
    b, s, _ = x3.shape
    tm = PROJ_TM
    row = pl.BlockSpec((1, tm, GROUP_W), lambda bi, i: (bi, i, 0))
    col = pl.BlockSpec((1, GROUP_W, tm), lambda bi, i: (bi, 0, i))
    aug = pl.BlockSpec((1, tm, LANES), lambda bi, i: (bi, i, 0))
    aug_t = pl.BlockSpec((1, LANES, tm), lambda bi, i: (bi, 0, i))
    stacked = [g, w_all, *gains, bf]
    small = stacked + list(consts)
    return pl.pallas_call(
        _proj_kernel,
        grid=(b, s // tm),
        in_specs=[pl.BlockSpec((1, tm, D_MODEL), lambda bi, i: (bi, i, 0))]
        + [_layer(a.shape, l) for a in stacked] + [_resident(a.shape) for a in consts],
        out_specs=[col, row, col, col, row, col, aug_t, aug],
        out_shape=[jax.ShapeDtypeStruct((b, GROUP_W, s), _BF16),
                   jax.ShapeDtypeStruct((b, s, GROUP_W), _BF16),
                   jax.ShapeDtypeStruct((b, GROUP_W, s), _BF16),
                   jax.ShapeDtypeStruct((b, GROUP_W, s), _BF16),
                   jax.ShapeDtypeStruct((b, s, GROUP_W), _BF16),
                   jax.ShapeDtypeStruct((b, GROUP_W, s), _BF16),
                   jax.ShapeDtypeStruct((b, LANES, s), _BF16),
                   jax.ShapeDtypeStruct((b, s, LANES), _BF16)],
        scratch_shapes=[pltpu.VMEM((1, LANES), _F32)],
        compiler_params=pltpu.CompilerParams(
            dimension_semantics=("arbitrary", "arbitrary"), vmem_limit_bytes=VMEM_LIMIT),
        name="proj",
    )(x3, *small)


class _HeadGroup:
    def __init__(self, qt_ref, k_ref, vt_ref, aqt_ref, ak_ref, o_ref, qcat_ref, s_ref, acc_ref,
                 kmean_ref=None):
        self.moba = kmean_ref is not None
        self.batched_aug = len(ak_ref.shape) == 3
        (self.qt_ref, self.k_ref, self.vt_ref, self.aqt_ref, self.ak_ref, self.o_ref,
         self.qcat_ref, self.s_ref, self.acc_ref, self.kmean_ref) = (
            qt_ref, k_ref, vt_ref, aqt_ref, ak_ref, o_ref, qcat_ref, s_ref, acc_ref, kmean_ref)
        self.n_super = k_ref.shape[1] // (2 * ATT_T)
        self.stages = [(sb, j) for sb in range(self.n_super) for j in range(sb + 1)]
        self.chains = [(half, hh) for half in range(2) for hh in range(2)]
        self.m_run = [None] * 4
        self.ones_rows = jnp.ones((SUM_ROWS, 2 * ATT_T), _BF16)

    def k_rows(self, start, n):
        ak = (self.ak_ref[0, start:start + n, :] if self.batched_aug
              else self.ak_ref[start:start + n, :])
        return jnp.concatenate([self.k_ref[0, start:start + n, :], ak], axis=1)

    def vt_cols(self, hh, start, n):
        vt = self.vt_ref[0, hh * HEAD_DIM:(hh + 1) * HEAD_DIM, start:start + n]
        return jnp.concatenate([vt, self.ones_rows[:, :n]], axis=0)

    def prologue(self):
        t = ATT_T
        row = lax.broadcasted_iota(jnp.int32, (LANES, 1), 0)
        qt = self.qt_ref[0]
        aqt = self.aqt_ref[0] if self.batched_aug else self.aqt_ref[...]
        zeros = jnp.zeros((HEAD_DIM, qt.shape[1]), _BF16)
        if self.moba:
            nb = self.kmean_ref.shape[0]
            for n in range(nb):
                blk = self.k_ref[0, n * MOBA_BLOCK:(n + 1) * MOBA_BLOCK, :].astype(_F32)
                self.kmean_ref[n:n + 1, :] = jnp.mean(blk, axis=0, keepdims=True)
            km = jnp.concatenate(_split3(self.kmean_ref[...]), axis=0)
        for hh in range(2):
            aug_lo = (2 * pl.program_id(1) + hh) * HEAD_AUG
            in_aug = (row >= aug_lo) & (row < aug_lo + HEAD_AUG)
            qa = jnp.where(in_aug, aqt, jnp.zeros_like(aqt))
            own = qt[hh * HEAD_DIM:(hh + 1) * HEAD_DIM]
            qh = jnp.concatenate([own, zeros] if hh == 0 else [zeros, own], axis=0)
            if self.moba:
                g3 = jnp.dot(km, qh, preferred_element_type=_F32)
                gate = g3[:nb] + g3[nb:2 * nb] + g3[2 * nb:]
                blk_id = lax.broadcasted_iota(jnp.int32, gate.shape, 0)
                n_past = lax.broadcasted_iota(jnp.int32, gate.shape, 1) // MOBA_BLOCK
                beaten = jnp.zeros(gate.shape, jnp.int32)
                for m in range(nb):
                    gm = gate[m:m + 1, :]
                    wins = (gm > gate) | ((gm == gate) & (blk_id > m))
                    beaten = beaten + jnp.where(wins & (n_past > m), 1, 0)
                keep = ((beaten < MOBA_TOPK) & (blk_id < n_past)) | (blk_id == n_past)
                selb = jnp.where(keep, 0.0, NEG)
                in_sel = (row >= aug_lo + SEL_LANE) & (row < aug_lo + SEL_LANE + nb)
                qa = qa + jnp.where(in_sel, jnp.tile(selb, (LANES // nb, 1)), 0.0).astype(_BF16)
            for sb in range(self.n_super):
                for half in range(2):
                    r0 = (2 * sb + half) * t
                    self.qcat_ref[sb, 2 * half + hh] = jnp.concatenate(
                        [qh[:, r0:r0 + t], qa[:, r0:r0 + t]], axis=0)

    def issue_scores(self, n, c):
        if n >= len(self.stages):
            return
        t = ATT_T
        sb, j = self.stages[n]
        half = self.chains[c][0]
        nk = t if (j == sb and half == 0) else 2 * t
        kk = self.k_rows(2 * j * t, nk)
        self.s_ref[n % 2, c, :nk, :] = jnp.dot(kk, self.qcat_ref[sb, c],
                                               preferred_element_type=_F32)

    def process(self, n, c):
        t = ATT_T
        sb, j = self.stages[n]
        half, hh = self.chains[c]
        diag = j == sb
        if diag:
            nk = (half + 1) * t
            s = self.s_ref[n % 2, c, :nk, :]
            key_pos = lax.broadcasted_iota(jnp.int32, s.shape, 0)
            qry_pos = lax.broadcasted_iota(jnp.int32, s.shape, 1) + half * t
            s = jnp.where(key_pos <= qry_pos, s, NEG)
        else:
            nk = 2 * t
            s = self.s_ref[n % 2, c]
        vv = self.vt_cols(hh, 2 * j * t, nk)
        m_blk = jnp.max(s, axis=0, keepdims=True)
        if j == 0:
            m_new = m_blk
            p = jnp.exp2(s - m_new).astype(_BF16)
            self.acc_ref[c] = jnp.dot(vv, p, preferred_element_type=_F32)
        else:
            m_new = jnp.maximum(self.m_run[c], m_blk)
            alpha = jnp.exp2(self.m_run[c] - m_new)
            p = jnp.exp2(s - m_new).astype(_BF16)
            self.acc_ref[c] = alpha * self.acc_ref[c] + jnp.dot(vv, p,
                                                                preferred_element_type=_F32)
        self.m_run[c] = m_new

    def finish_stage(self, n):
        t = ATT_T
        sb, j = self.stages[n]
        if j != sb:
            return
        for half in range(2):
            a0 = self.acc_ref[2 * half]
            a1 = self.acc_ref[2 * half + 1]
            out_t = jnp.concatenate([a0[:HEAD_DIM] / a0[HEAD_DIM:HEAD_DIM + 1],
                                     a1[:HEAD_DIM] / a1[HEAD_DIM:HEAD_DIM + 1]],
                                    axis=0)
            r0 = (2 * sb + half) * t
            self.o_ref[0, r0:r0 + t, :] = out_t.T.astype(_BF16)


def _attn_kernel(fqt_ref, fk_ref, fvt_ref, faqt_ref, fak_ref, mqt_ref, mk_ref, mvt_ref, maqt_ref,
                 mak_ref, fo_ref, mo_ref, fqcat_ref, fs_ref, facc_ref, mqcat_ref, ms_ref,
                 macc_ref, kmean_ref):
    groups = [_HeadGroup(fqt_ref, fk_ref, fvt_ref, faqt_ref, fak_ref, fo_ref, fqcat_ref, fs_ref,
                         facc_ref),
              _HeadGroup(mqt_ref, mk_ref, mvt_ref, maqt_ref, mak_ref, mo_ref, mqcat_ref, ms_ref,
                         macc_ref, kmean_ref)]
    for g in groups:
        g.prologue()
    for c in range(4):
        for g in groups:
            g.issue_scores(0, c)
    for n in range(len(groups[0].stages)):
        for c in range(4):
            for g in groups:
                g.issue_scores(n + 1, c)
                g.process(n, c)
        for g in groups:
            g.finish_stage(n)


def _attention(fqt, fk, fvt, faqt, fak, mqt, mk, mvt, maqt, mak):
    b, s, _ = fk.shape
    t = ATT_T
    npairs = GROUP_W // LANES
    seq_blk = pl.BlockSpec((1, s, LANES), lambda bi, p: (bi, 0, p))
    vt_blk = pl.BlockSpec((1, LANES, s), lambda bi, p: (bi, p, 0))
    faug_blk = pl.BlockSpec((1, s, LANES), lambda bi, p: (bi, 0, 0))
    faugt_blk = pl.BlockSpec((1, LANES, s), lambda bi, p: (bi, 0, 0))
    maug_blk = pl.BlockSpec((s, LANES), lambda bi, p: (0, 0))
    maugt_blk = pl.BlockSpec((LANES, s), lambda bi, p: (0, 0))
    group_scratch = [pltpu.VMEM((s // (2 * t), 4, 2 * LANES, t), _BF16),
                     pltpu.VMEM((2, 4, 2 * t, t), _F32),
                     pltpu.VMEM((4, HEAD_DIM + SUM_ROWS, t), _F32)]
    out = jax.ShapeDtypeStruct((b, s, GROUP_W), _BF16)
    return pl.pallas_call(
        _attn_kernel,
        grid=(b, npairs),
        in_specs=[vt_blk, seq_blk, vt_blk, faugt_blk, faug_blk,
                  vt_blk, seq_blk, vt_blk, maugt_blk, maug_blk],
        out_specs=[seq_blk, seq_blk],
        out_shape=[out, out],
        scratch_shapes=group_scratch + group_scratch
        + [pltpu.VMEM((s // MOBA_BLOCK, LANES), _F32)],
        compiler_params=pltpu.CompilerParams(
            dimension_semantics=("arbitrary", "arbitrary"), vmem_limit_bytes=VMEM_LIMIT),
        name="attn",
    )(fqt, fk, fvt, faqt, fak, mqt, mk, mvt, maqt, mak)


def _aug_base(h):
    return h * HEAD_AUG


def _np_split3(x):
    x = np.asarray(x, np.float32)
    hi = x.astype(ml_dtypes.bfloat16)
    r1 = x - hi.astype(np.float32)
    mid = r1.astype(ml_dtypes.bfloat16)
    lo = (r1 - mid.astype(np.float32)).astype(ml_dtypes.bfloat16)
    return hi, mid, lo


def _constants(seq):
    r = np.arange(PROJ_TM)
    tri = (r[None, :] <= r[:, None]).astype(np.float32)
    e = np.zeros((3 * LANES, 2 * LANES), np.float32)
    onesq = np.zeros((1, LANES), np.float32)
    onesk = np.zeros((1, LANES), np.float32)
    for h in range(N_FOX):
        base = _aug_base(h)
        for piece in range(3):
            e[piece * LANES + h, base + piece] = 1.0
            e[piece * LANES + h, LANES + base + 3 + piece] = -1.0
            onesq[0, base + 3 + piece] = 1.0
            onesk[0, base + piece] = 1.0
    pos = np.arange(seq, dtype=np.float32)
    taq = np.zeros((seq, LANES), ml_dtypes.bfloat16)
    tak = np.zeros((seq, LANES), ml_dtypes.bfloat16)
    for h in range(N_MOBA):
        base = _aug_base(h)
        slope = np.float32(2.0) ** np.float32(-8.0 * (h + 1) / N_MOBA)
        ramp = slope * pos * np.float32(LOG2E)
        for piece, (qv, kv) in enumerate(zip(_np_split3(-ramp), _np_split3(ramp))):
            taq[:, base + piece] = qv
            tak[:, base + 3 + piece] = kv
            taq[:, base + 3 + piece] = 1.0
            tak[:, base + piece] = 1.0
        for n in range(seq // MOBA_BLOCK):
            tak[n * MOBA_BLOCK:(n + 1) * MOBA_BLOCK, base + SEL_LANE + n] = 1.0
    bf = lambda a: jnp.asarray(a, _BF16)
    return ((bf(tri), bf(e), jnp.asarray(onesq), jnp.asarray(onesk)),
            (jnp.asarray(np.ascontiguousarray(taq.T)), jnp.asarray(tak)))


def _prep_w_in(w_in):
    w = GROUP_W
    o = 3 * w + N_FOX
    pad = jnp.zeros(w_in.shape[:2] + (LANES - N_FOX,), w_in.dtype)
    cols = [w_in[..., :3 * w], w_in[..., o:o + 3 * w], w_in[..., 3 * w:o], pad]
    return jnp.concatenate(cols, axis=-1).astype(_BF16)


def kernel(x, ffn1_norm_g, ffn1_w_gu, ffn1_w_down, mix_norm_g, w_in, b_f, fox_q_norm_g,
           fox_k_norm_g, moba_q_norm_g, moba_k_norm_g, w_out, ffn2_norm_g, ffn2_w_gu,
           ffn2_w_down):
    b, s, d = x.shape
    depth = w_in.shape[0]
    assert d == D_MODEL and s % (2 * ATT_T) == 0 and (b * s) % FFN_TM == 0
    proj_consts, (taqt, tak) = _constants(s)
    q_scale = np.float32(HEAD_DIM ** -0.5 * LOG2E)
    row = lambda g: g[:, None, :]

    ffn1 = (row(ffn1_norm_g), ffn1_w_gu.astype(_BF16), ffn1_w_down.astype(_BF16))
    ffn2 = (row(ffn2_norm_g), ffn2_w_gu.astype(_BF16), ffn2_w_down.astype(_BF16))
    w_out_b = w_out.astype(_BF16)
    w_all = _prep_w_in(w_in)
    col = lambda g: g[:, :, None]
    gains = [col(fox_q_norm_g) * q_scale, col(fox_k_norm_g),
             col(moba_q_norm_g) * q_scale, col(moba_k_norm_g)]
    bf = jnp.pad(b_f, ((0, 0), (0, LANES - N_FOX)))[:, None, :]
    mix_g = row(mix_norm_g)

    x2 = x.reshape(b * s, d)
    for l in range(depth):
        x2 = _ffn(x2, l, *ffn1)
        fqt, fk, fvt, mqt, mk, mvt, faqt, fak = _proj(
            x2.reshape(b, s, d), l, mix_g, w_all, gains, bf, proj_consts)
        fo, mo = _attention(fqt, fk, fvt, faqt, fak, mqt, mk, mvt, taqt, tak)
        x2 = _ffn(x2, l, *ffn2,
                  out_proj=(fo.reshape(b * s, GROUP_W), mo.reshape(b * s, GROUP_W), w_out_b))
    return x2.reshape(b, s, d)
```

```python
import functools

import numpy as np
import ml_dtypes
import jax
import jax.numpy as jnp
from jax import lax
from jax.experimental import pallas as pl
from jax.experimental.pallas import tpu as pltpu

D_MODEL = 1024
HEAD_DIM = 64
N_FOX = 8
N_MOBA = 8
GROUP_W = 512
D_FF = 2816
MOBA_BLOCK = 256
MOBA_TOPK = 3
RMS_EPS = 1e-6
NEG = -1e30
LOG2E = 1.4426950408889634

LANES = 128
FF_CHUNK = 768
FF_HEAD = D_FF % FF_CHUNK
N_FF_CHUNKS = D_FF // FF_CHUNK
ATT_T = 256
HEAD_AUG = 16
SEL_LANE = 8
SUM_ROWS = 16
FFN_TM = 1024
PROJ_TM = 512
VMEM_LIMIT = 56 * 1024 * 1024

_BF16 = jnp.bfloat16
_F32 = jnp.float32


def _resident(shape):
    nd = len(shape)
    return pl.BlockSpec(shape, lambda *_: (0,) * nd, pipeline_mode=pl.Buffered(1))


def _layer(shape, l):
    nd = len(shape) - 1
    return pl.BlockSpec((None,) + tuple(shape[1:]), lambda *_: (l,) + (0,) * nd,
                        pipeline_mode=pl.Buffered(1))


def _rms(x, g):
    return x * lax.rsqrt(jnp.mean(x * x, axis=-1, keepdims=True) + RMS_EPS) * g


def _split3(x):
    hi = x.astype(_BF16)
    r1 = x - hi.astype(_F32)
    mid = r1.astype(_BF16)
    lo = (r1 - mid.astype(_F32)).astype(_BF16)
    return hi, mid, lo


def _ffn_kernel(*refs, with_out_proj):
    if with_out_proj:
        x_ref, fo_ref, mo_ref, wout_ref, g_ref, wgu_ref, wd_ref, o_ref, h_ref = refs
        mixed = jnp.concatenate([fo_ref[...], mo_ref[...]], axis=1)
        x = x_ref[...] + jnp.dot(mixed, wout_ref[...], preferred_element_type=_F32)
        o_ref[...] = x
        res_ref = o_ref
    else:
        x_ref, g_ref, wgu_ref, wd_ref, o_ref, h_ref = refs
        x = x_ref[...]
        res_ref = x_ref
    h_ref[...] = _rms(x, g_ref[...]).astype(_BF16)

    def chunk(lo, width):
        h = h_ref[...]
        gate = jnp.dot(h, wgu_ref[:, pl.ds(lo, width)], preferred_element_type=_F32)
        up = jnp.dot(h, wgu_ref[:, pl.ds(D_FF + lo, width)], preferred_element_type=_F32)
        act = (gate * jax.nn.sigmoid(gate) * up).astype(_BF16)
        return jnp.dot(act, wd_ref[pl.ds(lo, width), :], preferred_element_type=_F32)

    acc = chunk(0, FF_HEAD)
    for c in range(N_FF_CHUNKS):
        acc = acc + chunk(FF_HEAD + c * FF_CHUNK, FF_CHUNK)
    o_ref[...] = res_ref[...] + 0.5 * acc


def _ffn(x2, l, g, wgu, wd, out_proj=None):
    n = x2.shape[0]
    tm = FFN_TM
    tok = lambda w: pl.BlockSpec((tm, w), lambda i: (i, 0))
    in_specs = [tok(D_MODEL)]
    args = [x2]
    if out_proj is not None:
        fo, mo, wout = out_proj
        in_specs += [tok(GROUP_W), tok(GROUP_W), _layer(wout.shape, l)]
        args += [fo, mo, wout]
    in_specs += [_layer(g.shape, l), _layer(wgu.shape, l), _layer(wd.shape, l)]
    args += [g, wgu, wd]
    return pl.pallas_call(
        functools.partial(_ffn_kernel, with_out_proj=out_proj is not None),
        grid=(n // tm,),
        in_specs=in_specs,
        out_specs=tok(D_MODEL),
        out_shape=jax.ShapeDtypeStruct((n, D_MODEL), _F32),
        scratch_shapes=[pltpu.VMEM((tm, D_MODEL), _BF16)],
        compiler_params=pltpu.CompilerParams(
            dimension_semantics=("arbitrary",), vmem_limit_bytes=VMEM_LIMIT),
        name="ffn_out" if out_proj is not None else "ffn",
    )(*args)


def _proj_kernel(x_ref, g_ref, w_ref, gfq_ref, gfk_ref, gmq_ref, gmk_ref, bf_ref,
                 tri_ref, e_ref, onesq_ref, onesk_ref,
                 fqt_ref, fk_ref, fvt_ref, mqt_ref, mk_ref, mvt_ref, faqt_ref, fak_ref,
                 carry_ref):
    @pl.when(pl.program_id(1) == 0)
    def _():
        carry_ref[...] = jnp.zeros_like(carry_ref)

    h = _rms(x_ref[0], g_ref[...]).astype(_BF16)
    proj = jnp.dot(h, w_ref[...], preferred_element_type=_F32)

    def head_norm_t(t, gain_ref):
        tt = t.T
        heads = []
        for hd in range(GROUP_W // HEAD_DIM):
            blk = tt[hd * HEAD_DIM:(hd + 1) * HEAD_DIM]
            ms = jnp.mean(blk * blk, axis=0, keepdims=True)
            heads.append(blk * lax.rsqrt(ms + RMS_EPS) * gain_ref[...])
        return jnp.concatenate(heads, axis=0)

    w = GROUP_W
    fqt_ref[0] = head_norm_t(proj[:, 0 * w:1 * w], gfq_ref).astype(_BF16)
    fk_ref[0] = head_norm_t(proj[:, 1 * w:2 * w], gfk_ref).T.astype(_BF16)
    fvt_ref[0] = proj[:, 2 * w:3 * w].T.astype(_BF16)
    mqt_ref[0] = head_norm_t(proj[:, 3 * w:4 * w], gmq_ref).astype(_BF16)
    mk_ref[0] = head_norm_t(proj[:, 4 * w:5 * w], gmk_ref).T.astype(_BF16)
    mvt_ref[0] = proj[:, 5 * w:6 * w].T.astype(_BF16)

    z = proj[:, 6 * w:] + bf_ref[...]
    logf = jnp.minimum(z, 0.0) - jnp.log(1.0 + jnp.exp(-jnp.abs(z)))
    lane = lax.broadcasted_iota(jnp.int32, logf.shape, 1)
    logf = jnp.where(lane < N_FOX, logf, 0.0)
    pieces = jnp.concatenate(_split3(logf), axis=1)
    part = jnp.dot(tri_ref[...], pieces, preferred_element_type=_F32)
    cum = part[:, :LANES] + part[:, LANES:2 * LANES] + part[:, 2 * LANES:] + carry_ref[...]
    carry_ref[...] = cum[-1:, :]
    cpieces = jnp.concatenate(_split3(cum * LOG2E), axis=1)
    aug = jnp.dot(cpieces, e_ref[...], preferred_element_type=_F32)
    faqt_ref[0] = (aug[:, :LANES] + onesq_ref[...]).T.astype(_BF16)
    fak_ref[0] = (aug[:, LANES:] + onesk_ref[...]).astype(_BF16)


def _proj(x3, l, g, w_all, gains, bf, consts):
    b, s, _ = x3.shape
    tm = PROJ_TM
    row = pl.BlockSpec((1, tm, GROUP_W), lambda bi, i: (bi, i, 0))
    col = pl.BlockSpec((1, GROUP_W, tm), lambda bi, i: (bi, 0, i))
    aug = pl.BlockSpec((1, tm, LANES), lambda bi, i: (bi, i, 0))
    aug_t = pl.BlockSpec((1, LANES, tm), lambda bi, i: (bi, 0, i))
    stacked = [g, w_all, *gains, bf]
    small = stacked + list(consts)
    return pl.pallas_call(
        _proj_kernel,
        grid=(b, s // tm),
        in_specs=[pl.BlockSpec((1, tm, D_MODEL), lambda bi, i: (bi, i, 0))]
        + [_layer(a.shape, l) for a in stacked] + [_resident(a.shape) for a in consts],
        out_specs=[col, row, col, col, row, col, aug_t, aug],
        out_shape=[jax.ShapeDtypeStruct((b, GROUP_W, s), _BF16),
                   jax.ShapeDtypeStruct((b, s, GROUP_W), _BF16),
                   jax.ShapeDtypeStruct((b, GROUP_W, s), _BF16),
                   jax.ShapeDtypeStruct((b, GROUP_W, s), _BF16),
                   jax.ShapeDtypeStruct((b, s, GROUP_W), _BF16),
                   jax.ShapeDtypeStruct((b, GROUP_W, s), _BF16),
                   jax.ShapeDtypeStruct((b, LANES, s), _BF16),
                   jax.ShapeDtypeStruct((b, s, LANES), _BF16)],
        scratch_shapes=[pltpu.VMEM((1, LANES), _F32)],
        compiler_params=pltpu.CompilerParams(
            dimension_semantics=("arbitrary", "arbitrary"), vmem_limit_bytes=VMEM_LIMIT),
        name="proj",
    )(x3, *small)


class _HeadGroup:
    def __init__(self, qt_ref, k_ref, vt_ref, aqt_ref, ak_ref, o_ref, qcat_ref, s_ref, acc_ref,
                 kmean_ref=None):
        self.moba = kmean_ref is not None
        self.batched_aug = len(ak_ref.shape) == 3
        (self.qt_ref, self.k_ref, self.vt_ref, self.aqt_ref, self.ak_ref, self.o_ref,
         self.qcat_ref, self.s_ref, self.acc_ref, self.kmean_ref) = (
            qt_ref, k_ref, vt_ref, aqt_ref, ak_ref, o_ref, qcat_ref, s_ref, acc_ref, kmean_ref)
        self.n_super = k_ref.shape[1] // (2 * ATT_T)
        self.stages = [(sb, j) for sb in range(self.n_super) for j in range(sb + 1)]
        self.chains = [(half, hh) for half in range(2) for hh in range(2)]
        self.m_run = [None] * 4
        self.ones_rows = jnp.ones((SUM_ROWS, 2 * ATT_T), _BF16)

    def k_rows(self, start, n):
        ak = (self.ak_ref[0, start:start + n, :] if self.batched_aug
              else self.ak_ref[start:start + n, :])
        return jnp.concatenate([self.k_ref[0, start:start + n, :], ak], axis=1)

    def vt_cols(self, hh, start, n):
        vt = self.vt_ref[0, hh * HEAD_DIM:(hh + 1) * HEAD_DIM, start:start + n]
        return jnp.concatenate([vt, self.ones_rows[:, :n]], axis=0)

    def prologue(self):
        t = ATT_T
        row = lax.broadcasted_iota(jnp.int32, (LANES, 1), 0)
        qt = self.qt_ref[0]
        aqt = self.aqt_ref[0] if self.batched_aug else self.aqt_ref[...]
        zeros = jnp.zeros((HEAD_DIM, qt.shape[1]), _BF16)
        if self.moba:
            nb = self.kmean_ref.shape[0]
            for n in range(nb):
                blk = self.k_ref[0, n * MOBA_BLOCK:(n + 1) * MOBA_BLOCK, :].astype(_F32)
                self.kmean_ref[n:n + 1, :] = jnp.mean(blk, axis=0, keepdims=True)
            km = jnp.concatenate(_split3(self.kmean_ref[...]), axis=0)
        for hh in range(2):
            aug_lo = (2 * pl.program_id(1) + hh) * HEAD_AUG
            in_aug = (row >= aug_lo) & (row < aug_lo + HEAD_AUG)
            qa = jnp.where(in_aug, aqt, jnp.zeros_like(aqt))
            own = qt[hh * HEAD_DIM:(hh + 1) * HEAD_DIM]
            qh = jnp.concatenate([own, zeros] if hh == 0 else [zeros, own], axis=0)
            if self.moba:
                g3 = jnp.dot(km, qh, preferred_element_type=_F32)
                gate = g3[:nb] + g3[nb:2 * nb] + g3[2 * nb:]
                blk_id = lax.broadcasted_iota(jnp.int32, gate.shape, 0)
                n_past = lax.broadcasted_iota(jnp.int32, gate.shape, 1) // MOBA_BLOCK
                beaten = jnp.zeros(gate.shape, jnp.int32)
                for m in range(nb):
                    gm = gate[m:m + 1, :]
                    wins = (gm > gate) | ((gm == gate) & (blk_id > m))
                    beaten = beaten + jnp.where(wins & (n_past > m), 1, 0)
                keep = ((beaten < MOBA_TOPK) & (blk_id < n_past)) | (blk_id == n_past)
                selb = jnp.where(keep, 0.0, NEG)
                in_sel = (row >= aug_lo + SEL_LANE) & (row < aug_lo + SEL_LANE + nb)
                qa = qa + jnp.where(in_sel, jnp.tile(selb, (LANES // nb, 1)), 0.0).astype(_BF16)
            for sb in range(self.n_super):
                for half in range(2):
                    r0 = (2 * sb + half) * t
                    self.qcat_ref[sb, 2 * half + hh] = jnp.concatenate(
                        [qh[:, r0:r0 + t], qa[:, r0:r0 + t]], axis=0)

    def issue_scores(self, n, c):
        if n >= len(self.stages):
            return
        t = ATT_T
        sb, j = self.stages[n]
        half = self.chains[c][0]
        nk = t if (j == sb and half == 0) else 2 * t
        kk = self.k_rows(2 * j * t, nk)
        self.s_ref[n % 2, c, :nk, :] = jnp.dot(kk, self.qcat_ref[sb, c],
                                               preferred_element_type=_F32)

    def process(self, n, c):
        t = ATT_T
        sb, j = self.stages[n]
        half, hh = self.chains[c]
        diag = j == sb
        if diag:
            nk = (half + 1) * t
            s = self.s_ref[n % 2, c, :nk, :]
            key_pos = lax.broadcasted_iota(jnp.int32, s.shape, 0)
            qry_pos = lax.broadcasted_iota(jnp.int32, s.shape, 1) + half * t
            s = jnp.where(key_pos <= qry_pos, s, NEG)
        else:
            nk = 2 * t
            s = self.s_ref[n % 2, c]
        vv = self.vt_cols(hh, 2 * j * t, nk)
        m_blk = jnp.max(s, axis=0, keepdims=True)
        if j == 0:
            m_new = m_blk
            p = jnp.exp2(s - m_new).astype(_BF16)
            self.acc_ref[c] = jnp.dot(vv, p, preferred_element_type=_F32)
        else:
            m_new = jnp.maximum(self.m_run[c], m_blk)
            alpha = jnp.exp2(self.m_run[c] - m_new)
            p = jnp.exp2(s - m_new).astype(_BF16)
            self.acc_ref[c] = alpha * self.acc_ref[c] + jnp.dot(vv, p,
                                                                preferred_element_type=_F32)
        self.m_run[c] = m_new

    def finish_stage(self, n):
        t = ATT_T
        sb, j = self.stages[n]
        if j != sb:
            return
        for half in range(2):
            a0 = self.acc_ref[2 * half]
            a1 = self.acc_ref[2 * half + 1]
            out_t = jnp.concatenate([a0[:HEAD_DIM] / a0[HEAD_DIM:HEAD_DIM + 1],
                                     a1[:HEAD_DIM] / a1[HEAD_DIM:HEAD_DIM + 1]],
                                    axis=0)
            r0 = (2 * sb + half) * t
            self.o_ref[0, r0:r0 + t, :] = out_t.T.astype(_BF16)


def _attn_kernel(fqt_ref, fk_ref, fvt_ref, faqt_ref, fak_ref, mqt_ref, mk_ref, mvt_ref, maqt_ref,
                 mak_ref, fo_ref, mo_ref, fqcat_ref, fs_ref, facc_ref, mqcat_ref, ms_ref,
                 macc_ref, kmean_ref):
    groups = [_HeadGroup(fqt_ref, fk_ref, fvt_ref, faqt_ref, fak_ref, fo_ref, fqcat_ref, fs_ref,
                         facc_ref),
              _HeadGroup(mqt_ref, mk_ref, mvt_ref, maqt_ref, mak_ref, mo_ref, mqcat_ref, ms_ref,
                         macc_ref, kmean_ref)]
    for g in groups:
        g.prologue()
    for c in range(4):
        for g in groups:
            g.issue_scores(0, c)
    for n in range(len(groups[0].stages)):
        for c in range(4):
            for g in groups:
                g.issue_scores(n + 1, c)
                g.process(n, c)
        for g in groups:
            g.finish_stage(n)


def _attention(fqt, fk, fvt, faqt, fak, mqt, mk, mvt, maqt, mak):
    b, s, _ = fk.shape
    t = ATT_T
    npairs = GROUP_W // LANES
    seq_blk = pl.BlockSpec((1, s, LANES), lambda bi, p: (bi, 0, p))
    vt_blk = pl.BlockSpec((1, LANES, s), lambda bi, p: (bi, p, 0))
    faug_blk = pl.BlockSpec((1, s, LANES), lambda bi, p: (bi, 0, 0))
    faugt_blk = pl.BlockSpec((1, LANES, s), lambda bi, p: (bi, 0, 0))
    maug_blk = pl.BlockSpec((s, LANES), lambda bi, p: (0, 0))
    maugt_blk = pl.BlockSpec((LANES, s), lambda bi, p: (0, 0))
    group_scratch = [pltpu.VMEM((s // (2 * t), 4, 2 * LANES, t), _BF16),
                     pltpu.VMEM((2, 4, 2 * t, t), _F32),
                     pltpu.VMEM((4, HEAD_DIM + SUM_ROWS, t), _F32)]
    out = jax.ShapeDtypeStruct((b, s, GROUP_W), _BF16)
    return pl.pallas_call(
        _attn_kernel,
        grid=(b, npairs),
        in_specs=[vt_blk, seq_blk, vt_blk, faugt_blk, faug_blk,
                  vt_blk, seq_blk, vt_blk, maugt_blk, maug_blk],
        out_specs=[seq_blk, seq_blk],
        out_shape=[out, out],
        scratch_shapes=group_scratch + group_scratch
        + [pltpu.VMEM((s // MOBA_BLOCK, LANES), _F32)],
        compiler_params=pltpu.CompilerParams(
            dimension_semantics=("arbitrary", "arbitrary"), vmem_limit_bytes=VMEM_LIMIT),
        name="attn",
    )(fqt, fk, fvt, faqt, fak, mqt, mk, mvt, maqt, mak)


def _aug_base(h):
    return h * HEAD_AUG


def _np_split3(x):
    x = np.asarray(x, np.float32)
    hi = x.astype(ml_dtypes.bfloat16)
    r1 = x - hi.astype(np.float32)
    mid = r1.astype(ml_dtypes.bfloat16)
    lo = (r1 - mid.astype(np.float32)).astype(ml_dtypes.bfloat16)
    return hi, mid, lo


def _constants(seq):
    r = np.arange(PROJ_TM)
    tri = (r[None, :] <= r[:, None]).astype(np.float32)
    e = np.zeros((3 * LANES, 2 * LANES), np.float32)
    onesq = np.zeros((1, LANES), np.float32)
    onesk = np.zeros((1, LANES), np.float32)
    for h in range(N_FOX):
        base = _aug_base(h)
        for piece in range(3):
            e[piece * LANES + h, base + piece] = 1.0
            e[piece * LANES + h, LANES + base + 3 + piece] = -1.0
            onesq[0, base + 3 + piece] = 1.0
            onesk[0, base + piece] = 1.0
    pos = np.arange(seq, dtype=np.float32)
    taq = np.zeros((seq, LANES), ml_dtypes.bfloat16)
    tak = np.zeros((seq, LANES), ml_dtypes.bfloat16)
    for h in range(N_MOBA):
        base = _aug_base(h)
        slope = np.float32(2.0) ** np.float32(-8.0 * (h + 1) / N_MOBA)
        ramp = slope * pos * np.float32(LOG2E)
        for piece, (qv, kv) in enumerate(zip(_np_split3(-ramp), _np_split3(ramp))):
            taq[:, base + piece] = qv
            tak[:, base + 3 + piece] = kv
            taq[:, base + 3 + piece] = 1.0
            tak[:, base + piece] = 1.0
        for n in range(seq // MOBA_BLOCK):
            tak[n * MOBA_BLOCK:(n + 1) * MOBA_BLOCK, base + SEL_LANE + n] = 1.0
    bf = lambda a: jnp.asarray(a, _BF16)
    return ((bf(tri), bf(e), jnp.asarray(onesq), jnp.asarray(onesk)),
            (jnp.asarray(np.ascontiguousarray(taq.T)), jnp.asarray(tak)))


def _prep_w_in(w_in):
    w = GROUP_W
    o = 3 * w + N_FOX
    pad = jnp.zeros(w_in.shape[:2] + (LANES - N_FOX,), w_in.dtype)
    cols = [w_in[..., :3 * w], w_in[..., o:o + 3 * w], w_in[..., 3 * w:o], pad]
    return jnp.concatenate(cols, axis=-1).astype(_BF16)


def kernel(x, ffn1_norm_g, ffn1_w_gu, ffn1_w_down, mix_norm_g, w_in, b_f, fox_q_norm_g,
           fox_k_norm_g, moba_q_norm_g, moba_k_norm_g, w_out, ffn2_norm_g, ffn2_w_gu,
           ffn2_w_down):
    b, s, d = x.shape
    depth = w_in.shape[0]
    assert d == D_MODEL and s % (2 * ATT_T) == 0 and (b * s) % FFN_TM == 0
    proj_consts, (taqt, tak) = _constants(s)
    q_scale = np.float32(HEAD_DIM ** -0.5 * LOG2E)
    row = lambda g: g[:, None, :]

    ffn1 = (row(ffn1_norm_g), ffn1_w_gu.astype(_BF16), ffn1_w_down.astype(_BF16))
    ffn2 = (row(ffn2_norm_g), ffn2_w_gu.astype(_BF16), ffn2_w_down.astype(_BF16))
    w_out_b = w_out.astype(_BF16)
    w_all = _prep_w_in(w_in)
    col = lambda g: g[:, :, None]
    gains = [col(fox_q_norm_g) * q_scale, col(fox_k_norm_g),
             col(moba_q_norm_g) * q_scale, col(moba_k_norm_g)]
    bf = jnp.pad(b_f, ((0, 0), (0, LANES - N_FOX)))[:, None, :]
    mix_g = row(mix_norm_g)

    x2 = x.reshape(b * s, d)
    for l in range(depth):
        x2 = _ffn(x2, l, *ffn1)
        fqt, fk, fvt, mqt, mk, mvt, faqt, fak = _proj(
            x2.reshape(b, s, d), l, mix_g, w_all, gains, bf, proj_consts)
        fo, mo = _attention(fqt, fk, fvt, faqt, fak, mqt, mk, mvt, taqt, tak)
        x2 = _ffn(x2, l, *ffn2,
                  out_proj=(fo.reshape(b * s, GROUP_W), mo.reshape(b * s, GROUP_W), w_out_b))
    return x2.reshape(b, s, d)
```

```python
import functools

import numpy as np
import ml_dtypes
import jax
import jax.numpy as jnp
from jax import lax
from jax.experimental import pallas as pl
from jax.experimental.pallas import tpu as pltpu

D_MODEL = 1024
HEAD_DIM = 64
N_FOX = 8
N_MOBA = 8
GROUP_W = 512
D_FF = 2816
MOBA_BLOCK = 256
MOBA_TOPK = 3
RMS_EPS = 1e-6
NEG = -1e30
LOG2E = 1.4426950408889634

LANES = 128
FF_CHUNK = 768
FF_HEAD = D_FF % FF_CHUNK
N_FF_CHUNKS = D_FF // FF_CHUNK
ATT_T = 256
HEAD_AUG = 16
SEL_LANE = 8
SUM_ROWS = 16
FFN_TM = 1024
PROJ_TM = 512
VMEM_LIMIT = 56 * 1024 * 1024

_BF16 = jnp.bfloat16
_F32 = jnp.float32


def _resident(shape):
    nd = len(shape)
    return pl.BlockSpec(shape, lambda *_: (0,) * nd, pipeline_mode=pl.Buffered(1))


def _layer(shape, l):
    nd = len(shape) - 1
    return pl.BlockSpec((None,) + tuple(shape[1:]), lambda *_: (l,) + (0,) * nd,
                        pipeline_mode=pl.Buffered(1))


def _rms(x, g):
    return x * lax.rsqrt(jnp.mean(x * x, axis=-1, keepdims=True) + RMS_EPS) * g


def _split3(x):
    hi = x.astype(_BF16)
    r1 = x - hi.astype(_F32)
    mid = r1.astype(_BF16)
    lo = (r1 - mid.astype(_F32)).astype(_BF16)
    return hi, mid, lo


def _ffn_kernel(*refs, with_out_proj):
    if with_out_proj:
        x_ref, fo_ref, mo_ref, wout_ref, g_ref, wgu_ref, wd_ref, o_ref, h_ref = refs
        mixed = jnp.concatenate([fo_ref[...], mo_ref[...]], axis=1)
        x = x_ref[...] + jnp.dot(mixed, wout_ref[...], preferred_element_type=_F32)
        o_ref[...] = x
        res_ref = o_ref
    else:
        x_ref, g_ref, wgu_ref, wd_ref, o_ref, h_ref = refs
        x = x_ref[...]
        res_ref = x_ref
    h_ref[...] = _rms(x, g_ref[...]).astype(_BF16)

    def chunk(lo, width):
        h = h_ref[...]
        gate = jnp.dot(h, wgu_ref[:, pl.ds(lo, width)], preferred_element_type=_F32)
        up = jnp.dot(h, wgu_ref[:, pl.ds(D_FF + lo, width)], preferred_element_type=_F32)
        act = (gate * jax.nn.sigmoid(gate) * up).astype(_BF16)
        return jnp.dot(act, wd_ref[pl.ds(lo, width), :], preferred_element_type=_F32)

    acc = chunk(0, FF_HEAD)
    for c in range(N_FF_CHUNKS):
        acc = acc + chunk(FF_HEAD + c * FF_CHUNK, FF_CHUNK)
    o_ref[...] = res_ref[...] + 0.5 * acc


def _ffn(x2, l, g, wgu, wd, out_proj=None):
    n = x2.shape[0]
    tm = FFN_TM
    tok = lambda w: pl.BlockSpec((tm, w), lambda i: (i, 0))
    in_specs = [tok(D_MODEL)]
    args = [x2]
    if out_proj is not None:
        fo, mo, wout = out_proj
        in_specs += [tok(GROUP_W), tok(GROUP_W), _layer(wout.shape, l)]
        args += [fo, mo, wout]
    in_specs += [_layer(g.shape, l), _layer(wgu.shape, l), _layer(wd.shape, l)]
    args += [g, wgu, wd]
    return pl.pallas_call(
        functools.partial(_ffn_kernel, with_out_proj=out_proj is not None),
        grid=(n // tm,),
        in_specs=in_specs,
        out_specs=tok(D_MODEL),
        out_shape=jax.ShapeDtypeStruct((n, D_MODEL), _F32),
        scratch_shapes=[pltpu.VMEM((tm, D_MODEL), _BF16)],
        compiler_params=pltpu.CompilerParams(
            dimension_semantics=("arbitrary",), vmem_limit_bytes=VMEM_LIMIT,
            allow_input_fusion=[a.dtype == _BF16 and a.ndim == 3 for a in args]),
        name="ffn_out" if out_proj is not None else "ffn",
    )(*args)


def _proj_kernel(x_ref, g_ref, w_ref, gfq_ref, gfk_ref, gmq_ref, gmk_ref, bf_ref,
                 tri_ref, e_ref, onesq_ref, onesk_ref,
                 fqt_ref, fk_ref, fvt_ref, mqt_ref, mk_ref, mvt_ref, faqt_ref, fak_ref,
                 carry_ref):
    @pl.when(pl.program_id(1) == 0)
    def _():
        carry_ref[...] = jnp.zeros_like(carry_ref)

    h = _rms(x_ref[0], g_ref[...]).astype(_BF16)
    proj = jnp.dot(h, w_ref[...], preferred_element_type=_F32)

    def head_norm_t(t, gain_ref):
        tt = t.T
        heads = []
        for hd in range(GROUP_W // HEAD_DIM):
            blk = tt[hd * HEAD_DIM:(hd + 1) * HEAD_DIM]
            ms = jnp.mean(blk * blk, axis=0, keepdims=True)
            heads.append(blk * lax.rsqrt(ms + RMS_EPS) * gain_ref[...])
        return jnp.concatenate(heads, axis=0)

    w = GROUP_W
    fqt_ref[0] = head_norm_t(proj[:, 0 * w:1 * w], gfq_ref).astype(_BF16)
    fk_ref[0] = head_norm_t(proj[:, 1 * w:2 * w], gfk_ref).T.astype(_BF16)
    fvt_ref[0] = proj[:, 2 * w:3 * w].T.astype(_BF16)
    mqt_ref[0] = head_norm_t(proj[:, 3 * w:4 * w], gmq_ref).astype(_BF16)
    mk_ref[0] = head_norm_t(proj[:, 4 * w:5 * w], gmk_ref).T.astype(_BF16)
    mvt_ref[0] = proj[:, 5 * w:6 * w].T.astype(_BF16)

    z = proj[:, 6 * w:] + bf_ref[...]
    logf = jnp.minimum(z, 0.0) - jnp.log(1.0 + jnp.exp(-jnp.abs(z)))
    lane = lax.broadcasted_iota(jnp.int32, logf.shape, 1)
    logf = jnp.where(lane < N_FOX, logf, 0.0)
    pieces = jnp.concatenate(_split3(logf), axis=1)
    part = jnp.dot(tri_ref[...], pieces, preferred_element_type=_F32)
    cum = part[:, :LANES] + part[:, LANES:2 * LANES] + part[:, 2 * LANES:] + carry_ref[...]
    carry_ref[...] = cum[-1:, :]
    cpieces = jnp.concatenate(_split3(cum * LOG2E), axis=1)
    aug = jnp.dot(cpieces, e_ref[...], preferred_element_type=_F32)
    faqt_ref[0] = (aug[:, :LANES] + onesq_ref[...]).T.astype(_BF16)
    fak_ref[0] = (aug[:, LANES:] + onesk_ref[...]).astype(_BF16)


def _proj(x3, l, g, w_all, gains, bf, consts):
    b, s, _ = x3.shape
    tm = PROJ_TM
    row = pl.BlockSpec((1, tm, GROUP_W), lambda bi, i: (bi, i, 0))
    col = pl.BlockSpec((1, GROUP_W, tm), lambda bi, i: (bi, 0, i))
    aug = pl.BlockSpec((1, tm, LANES), lambda bi, i: (bi, i, 0))
    aug_t = pl.BlockSpec((1, LANES, tm), lambda bi, i: (bi, 0, i))
    stacked = [g, w_all, *gains, bf]
    small = stacked + list(consts)
    return pl.pallas_call(
        _proj_kernel,
        grid=(b, s // tm),
        in_specs=[pl.BlockSpec((1, tm, D_MODEL), lambda bi, i: (bi, i, 0))]
        + [_layer(a.shape, l) for a in stacked] + [_resident(a.shape) for a in consts],
        out_specs=[col, row, col, col, row, col, aug_t, aug],
        out_shape=[jax.ShapeDtypeStruct((b, GROUP_W, s), _BF16),
                   jax.ShapeDtypeStruct((b, s, GROUP_W), _BF16),
                   jax.ShapeDtypeStruct((b, GROUP_W, s), _BF16),
                   jax.ShapeDtypeStruct((b, GROUP_W, s), _BF16),
                   jax.ShapeDtypeStruct((b, s, GROUP_W), _BF16),
                   jax.ShapeDtypeStruct((b, GROUP_W, s), _BF16),
                   jax.ShapeDtypeStruct((b, LANES, s), _BF16),
                   jax.ShapeDtypeStruct((b, s, LANES), _BF16)],
        scratch_shapes=[pltpu.VMEM((1, LANES), _F32)],
        compiler_params=pltpu.CompilerParams(
            dimension_semantics=("arbitrary", "arbitrary"), vmem_limit_bytes=VMEM_LIMIT),
        name="proj",
    )(x3, *small)


class _HeadGroup:
    def __init__(self, qt_ref, k_ref, vt_ref, aqt_ref, ak_ref, o_ref, qcat_ref, s_ref, acc_ref,
                 kmean_ref=None):
        self.moba = kmean_ref is not None
        self.batched_aug = len(ak_ref.shape) == 3
        (self.qt_ref, self.k_ref, self.vt_ref, self.aqt_ref, self.ak_ref, self.o_ref,
         self.qcat_ref, self.s_ref, self.acc_ref, self.kmean_ref) = (
            qt_ref, k_ref, vt_ref, aqt_ref, ak_ref, o_ref, qcat_ref, s_ref, acc_ref, kmean_ref)
        self.n_super = k_ref.shape[1] // (2 * ATT_T)
        self.stages = [(sb, j) for sb in range(self.n_super) for j in range(sb + 1)]
        self.chains = [(half, hh) for half in range(2) for hh in range(2)]
        self.m_run = [None] * 4
        self.ones_rows = jnp.ones((SUM_ROWS, 2 * ATT_T), _BF16)

    def k_rows(self, start, n):
        ak = (self.ak_ref[0, start:start + n, :] if self.batched_aug
              else self.ak_ref[start:start + n, :])
        return jnp.concatenate([self.k_ref[0, start:start + n, :], ak], axis=1)

    def vt_cols(self, hh, start, n):
        vt = self.vt_ref[0, hh * HEAD_DIM:(hh + 1) * HEAD_DIM, start:start + n]
        return jnp.concatenate([vt, self.ones_rows[:, :n]], axis=0)

    def prologue(self):
        t = ATT_T
        row = lax.broadcasted_iota(jnp.int32, (LANES, 1), 0)
        qt = self.qt_ref[0]
        aqt = self.aqt_ref[0] if self.batched_aug else self.aqt_ref[...]
        zeros = jnp.zeros((HEAD_DIM, qt.shape[1]), _BF16)
        if self.moba:
            nb = self.kmean_ref.shape[0]
            for n in range(nb):
                blk = self.k_ref[0, n * MOBA_BLOCK:(n + 1) * MOBA_BLOCK, :].astype(_F32)
                self.kmean_ref[n:n + 1, :] = jnp.mean(blk, axis=0, keepdims=True)
            km = jnp.concatenate(_split3(self.kmean_ref[...]), axis=0)
        for hh in range(2):
            aug_lo = (2 * pl.program_id(1) + hh) * HEAD_AUG
            in_aug = (row >= aug_lo) & (row < aug_lo + HEAD_AUG)
            qa = jnp.where(in_aug, aqt, jnp.zeros_like(aqt))
            own = qt[hh * HEAD_DIM:(hh + 1) * HEAD_DIM]
            qh = jnp.concatenate([own, zeros] if hh == 0 else [zeros, own], axis=0)
            if self.moba:
                g3 = jnp.dot(km, qh, preferred_element_type=_F32)
                gate = g3[:nb] + g3[nb:2 * nb] + g3[2 * nb:]
                blk_id = lax.broadcasted_iota(jnp.int32, gate.shape, 0)
                n_past = lax.broadcasted_iota(jnp.int32, gate.shape, 1) // MOBA_BLOCK
                beaten = jnp.zeros(gate.shape, jnp.int32)
                for m in range(nb):
                    gm = gate[m:m + 1, :]
                    wins = (gm > gate) | ((gm == gate) & (blk_id > m))
                    beaten = beaten + jnp.where(wins & (n_past > m), 1, 0)
                keep = ((beaten < MOBA_TOPK) & (blk_id < n_past)) | (blk_id == n_past)
                selb = jnp.where(keep, 0.0, NEG)
                in_sel = (row >= aug_lo + SEL_LANE) & (row < aug_lo + SEL_LANE + nb)
                qa = qa + jnp.where(in_sel, jnp.tile(selb, (LANES // nb, 1)), 0.0).astype(_BF16)
            for sb in range(self.n_super):
                for half in range(2):
                    r0 = (2 * sb + half) * t
                    self.qcat_ref[sb, 2 * half + hh] = jnp.concatenate(
                        [qh[:, r0:r0 + t], qa[:, r0:r0 + t]], axis=0)

    def issue_scores(self, n, c):
        if n >= len(self.stages):
            return
        t = ATT_T
        sb, j = self.stages[n]
        half = self.chains[c][0]
        nk = t if (j == sb and half == 0) else 2 * t
        kk = self.k_rows(2 * j * t, nk)
        self.s_ref[n % 2, c, :nk, :] = jnp.dot(kk, self.qcat_ref[sb, c],
                                               preferred_element_type=_F32)

    def process(self, n, c):
        t = ATT_T
        sb, j = self.stages[n]
        half, hh = self.chains[c]
        diag = j == sb
        if diag:
            nk = (half + 1) * t
            s = self.s_ref[n % 2, c, :nk, :]
            key_pos = lax.broadcasted_iota(jnp.int32, s.shape, 0)
            qry_pos = lax.broadcasted_iota(jnp.int32, s.shape, 1) + half * t
            s = jnp.where(key_pos <= qry_pos, s, NEG)
        else:
            nk = 2 * t
            s = self.s_ref[n % 2, c]
        vv = self.vt_cols(hh, 2 * j * t, nk)
        m_blk = jnp.max(s, axis=0, keepdims=True)
        if j == 0:
            m_new = m_blk
            p = jnp.exp2(s - m_new).astype(_BF16)
            self.acc_ref[c] = jnp.dot(vv, p, preferred_element_type=_F32)
        else:
            m_new = jnp.maximum(self.m_run[c], m_blk)
            alpha = jnp.exp2(self.m_run[c] - m_new)
            p = jnp.exp2(s - m_new).astype(_BF16)
            self.acc_ref[c] = alpha * self.acc_ref[c] + jnp.dot(vv, p,
                                                                preferred_element_type=_F32)
        self.m_run[c] = m_new

    def finish_stage(self, n):
        t = ATT_T
        sb, j = self.stages[n]
        if j != sb:
            return
        for half in range(2):
            a0 = self.acc_ref[2 * half]
            a1 = self.acc_ref[2 * half + 1]
            out_t = jnp.concatenate([a0[:HEAD_DIM] / a0[HEAD_DIM:HEAD_DIM + 1],
                                     a1[:HEAD_DIM] / a1[HEAD_DIM:HEAD_DIM + 1]],
                                    axis=0)
            r0 = (2 * sb + half) * t
            self.o_ref[0, r0:r0 + t, :] = out_t.T.astype(_BF16)


def _attn_kernel(fqt_ref, fk_ref, fvt_ref, faqt_ref, fak_ref, mqt_ref, mk_ref, mvt_ref, maqt_ref,
                 mak_ref, fo_ref, mo_ref, fqcat_ref, fs_ref, facc_ref, mqcat_ref, ms_ref,
                 macc_ref, kmean_ref):
    groups = [_HeadGroup(fqt_ref, fk_ref, fvt_ref, faqt_ref, fak_ref, fo_ref, fqcat_ref, fs_ref,
                         facc_ref),
              _HeadGroup(mqt_ref, mk_ref, mvt_ref, maqt_ref, mak_ref, mo_ref, mqcat_ref, ms_ref,
                         macc_ref, kmean_ref)]
    for g in groups:
        g.prologue()
    for c in range(4):
        for g in groups:
            g.issue_scores(0, c)
    for n in range(len(groups[0].stages)):
        for c in range(4):
            for g in groups:
                g.issue_scores(n + 1, c)
                g.process(n, c)
        for g in groups:
            g.finish_stage(n)


def _attention(fqt, fk, fvt, faqt, fak, mqt, mk, mvt, maqt, mak):
    b, s, _ = fk.shape
    t = ATT_T
    npairs = GROUP_W // LANES
    seq_blk = pl.BlockSpec((1, s, LANES), lambda bi, p: (bi, 0, p))
    vt_blk = pl.BlockSpec((1, LANES, s), lambda bi, p: (bi, p, 0))
    faug_blk = pl.BlockSpec((1, s, LANES), lambda bi, p: (bi, 0, 0))
    faugt_blk = pl.BlockSpec((1, LANES, s), lambda bi, p: (bi, 0, 0))
    maug_blk = pl.BlockSpec((s, LANES), lambda bi, p: (0, 0))
    maugt_blk = pl.BlockSpec((LANES, s), lambda bi, p: (0, 0))
    group_scratch = [pltpu.VMEM((s // (2 * t), 4, 2 * LANES, t), _BF16),
                     pltpu.VMEM((2, 4, 2 * t, t), _F32),
                     pltpu.VMEM((4, HEAD_DIM + SUM_ROWS, t), _F32)]
    out = jax.ShapeDtypeStruct((b, s, GROUP_W), _BF16)
    return pl.pallas_call(
        _attn_kernel,
        grid=(b, npairs),
        in_specs=[vt_blk, seq_blk, vt_blk, faugt_blk, faug_blk,
                  vt_blk, seq_blk, vt_blk, maugt_blk, maug_blk],
        out_specs=[seq_blk, seq_blk],
        out_shape=[out, out],
        scratch_shapes=group_scratch + group_scratch
        + [pltpu.VMEM((s // MOBA_BLOCK, LANES), _F32)],
        compiler_params=pltpu.CompilerParams(
            dimension_semantics=("arbitrary", "arbitrary"), vmem_limit_bytes=VMEM_LIMIT),
        name="attn",
    )(fqt, fk, fvt, faqt, fak, mqt, mk, mvt, maqt, mak)


def _aug_base(h):
    return h * HEAD_AUG


def _np_split3(x):
    x = np.asarray(x, np.float32)
    hi = x.astype(ml_dtypes.bfloat16)
    r1 = x - hi.astype(np.float32)
    mid = r1.astype(ml_dtypes.bfloat16)
    lo = (r1 - mid.astype(np.float32)).astype(ml_dtypes.bfloat16)
    return hi, mid, lo


def _constants(seq):
    r = np.arange(PROJ_TM)
    tri = (r[None, :] <= r[:, None]).astype(np.float32)
    e = np.zeros((3 * LANES, 2 * LANES), np.float32)
    onesq = np.zeros((1, LANES), np.float32)
    onesk = np.zeros((1, LANES), np.float32)
    for h in range(N_FOX):
        base = _aug_base(h)
        for piece in range(3):
            e[piece * LANES + h, base + piece] = 1.0
            e[piece * LANES + h, LANES + base + 3 + piece] = -1.0
            onesq[0, base + 3 + piece] = 1.0
            onesk[0, base + piece] = 1.0
    pos = np.arange(seq, dtype=np.float32)
    taq = np.zeros((seq, LANES), ml_dtypes.bfloat16)
    tak = np.zeros((seq, LANES), ml_dtypes.bfloat16)
    for h in range(N_MOBA):
        base = _aug_base(h)
        slope = np.float32(2.0) ** np.float32(-8.0 * (h + 1) / N_MOBA)
        ramp = slope * pos * np.float32(LOG2E)
        for piece, (qv, kv) in enumerate(zip(_np_split3(-ramp), _np_split3(ramp))):
            taq[:, base + piece] = qv
            tak[:, base + 3 + piece] = kv
            taq[:, base + 3 + piece] = 1.0
            tak[:, base + piece] = 1.0
        for n in range(seq // MOBA_BLOCK):
            tak[n * MOBA_BLOCK:(n + 1) * MOBA_BLOCK, base + SEL_LANE + n] = 1.0
    bf = lambda a: jnp.asarray(a, _BF16)
    return ((bf(tri), bf(e), jnp.asarray(onesq), jnp.asarray(onesk)),
            (jnp.asarray(np.ascontiguousarray(taq.T)), jnp.asarray(tak)))


def _prep_w_in(w_in):
    w = GROUP_W
    o = 3 * w + N_FOX
    pad = jnp.zeros(w_in.shape[:2] + (LANES - N_FOX,), w_in.dtype)
    cols = [w_in[..., :3 * w], w_in[..., o:o + 3 * w], w_in[..., 3 * w:o], pad]
    return jnp.concatenate(cols, axis=-1).astype(_BF16)


def kernel(x, ffn1_norm_g, ffn1_w_gu, ffn1_w_down, mix_norm_g, w_in, b_f, fox_q_norm_g,
           fox_k_norm_g, moba_q_norm_g, moba_k_norm_g, w_out, ffn2_norm_g, ffn2_w_gu,
           ffn2_w_down):
    b, s, d = x.shape
    depth = w_in.shape[0]
    assert d == D_MODEL and s % (2 * ATT_T) == 0 and (b * s) % FFN_TM == 0
    proj_consts, (taqt, tak) = _constants(s)
    q_scale = np.float32(HEAD_DIM ** -0.5 * LOG2E)
    row = lambda g: g[:, None, :]

    ffn1 = (row(ffn1_norm_g), ffn1_w_gu.astype(_BF16), ffn1_w_down.astype(_BF16))
    ffn2 = (row(ffn2_norm_g), ffn2_w_gu.astype(_BF16), ffn2_w_down.astype(_BF16))
    w_out_b = w_out.astype(_BF16)
    w_all = _prep_w_in(w_in)
    col = lambda g: g[:, :, None]
    gains = [col(fox_q_norm_g) * q_scale, col(fox_k_norm_g),
             col(moba_q_norm_g) * q_scale, col(moba_k_norm_g)]
    bf = jnp.pad(b_f, ((0, 0), (0, LANES - N_FOX)))[:, None, :]
    mix_g = row(mix_norm_g)

    x2 = x.reshape(b * s, d)
    for l in range(depth):
        x2 = _ffn(x2, l, *ffn1)
        fqt, fk, fvt, mqt, mk, mvt, faqt, fak = _proj(
            x2.reshape(b, s, d), l, mix_g, w_all, gains, bf, proj_consts)
        fo, mo = _attention(fqt, fk, fvt, faqt, fak, mqt, mk, mvt, taqt, tak)
        x2 = _ffn(x2, l, *ffn2,
                  out_proj=(fo.reshape(b * s, GROUP_W), mo.reshape(b * s, GROUP_W), w_out_b))
    return x2.reshape(b, s, d)
```
